```python
import math
import jax, jax.numpy as jnp
from jax import lax
import numpy as np

D_MODEL = 1024
BATCH = 16
SEQ = 2048
DEPTH = 2
DEC_BATCH = 32
DEC_SEQ = 4
PAST_LEN = 16384
PAGE_SIZE = 128

ATT_HEADS = 4
ATT_DH = 64
ATT_DV = 2 * ATT_DH
ATT_W = ATT_HEADS * ATT_DV
QK_W = ATT_HEADS * 2 * ATT_DH
CONV_W = 256
CONV_GROUPS = 4
CONV_K = 31
GM_HEADS = 4
GM_DH = 64
GM_W = GM_HEADS * GM_DH
CHUNK = 128
MIX_W = ATT_W + CONV_W + GM_W
IN_W = 2 * QK_W + ATT_W + 2 * CONV_W + 2 * GM_W
D_FF = -(-8 * D_MODEL // (3 * 256)) * 256
Q_BLOCK = 128
EPS = 1e-6

kernel_name = "hymba_conv_diffattn_chunkmlp_decode_step"


def _rms(x, g):
    xf = x.astype(jnp.float32)
    y = xf * lax.rsqrt(jnp.mean(xf * xf, axis=-1, keepdims=True) + EPS)
    return (y * g.astype(jnp.float32)).astype(x.dtype)


def _ln(x, g, b):
    xf = x.astype(jnp.float32)
    mu = jnp.mean(xf, axis=-1, keepdims=True)
    var = jnp.mean(jnp.square(xf - mu), axis=-1, keepdims=True)
    y = (xf - mu) * lax.rsqrt(var + EPS)
    return (y * g.astype(jnp.float32) + b.astype(jnp.float32)).astype(x.dtype)


def _in_proj(x, norm_g, w_in, q_g, k_g):
    lead = x.shape[:-1]
    z = _rms(x, norm_g) @ w_in
    q, k, v, c, m = jnp.split(
        z, [QK_W, 2 * QK_W, 2 * QK_W + ATT_W, 2 * QK_W + ATT_W + 2 * CONV_W], axis=-1)
    q = _rms(q.reshape(*lead, ATT_HEADS, 2, ATT_DH), q_g)
    k = _rms(k.reshape(*lead, ATT_HEADS, 2, ATT_DH), k_g)
    v = v.reshape(*lead, ATT_HEADS, ATT_DV)
    return q, k, v, c, m


def _diff_lambda(lq1, lk1, lq2, lk2, lam_init):
    f = jnp.float32
    return (jnp.exp(jnp.sum(lq1.astype(f) * lk1.astype(f)))
            - jnp.exp(jnp.sum(lq2.astype(f) * lk2.astype(f))) + lam_init)


def _attn_prompt(q, k, v, lam):
    B, S = q.shape[:2]
    nb = S // Q_BLOCK
    qb = jnp.moveaxis(q.reshape(B, nb, Q_BLOCK, ATT_HEADS, 2, ATT_DH), 1, 0)
    kpos = jnp.arange(S)
    scale = ATT_DH ** -0.5

    def blk(args):
        qi, i = args
        qpos = i * Q_BLOCK + jnp.arange(Q_BLOCK)
        s = jnp.einsum('bqhnd,bkhnd->bnhqk', qi, k).astype(jnp.float32) * scale
        s = jnp.where(kpos[None, :] <= qpos[:, None], s, -jnp.inf)
        p = jax.nn.softmax(s, axis=-1)
        w = p[:, 0] - lam * p[:, 1]
        return jnp.einsum('bhqk,bkhe->bqhe', w.astype(v.dtype), v)

    o = lax.map(blk, (qb, jnp.arange(nb)))
    return jnp.moveaxis(o, 0, 1).reshape(B, S, ATT_HEADS, ATT_DV)


def _online_update(carry, s, vv):
    m, l, acc = carry
    m_new = jnp.maximum(m, jnp.max(s, axis=-1))
    c = jnp.exp(m - m_new)
    p = jnp.exp(s - m_new[..., None])
    l = l * c + jnp.sum(p, axis=-1)
    acc = acc * c[..., None] + jnp.einsum('bnhqk,bkhe->bnhqe', p, vv.astype(jnp.float32))
    return (m_new, l, acc)


def _attn_sample(q, k_new, v_new, cache_k, cache_v, layer, page_table, lam):
    Bd, T = q.shape[:2]
    scale = ATT_DH ** -0.5
    f32 = jnp.float32
    init = (jnp.full((Bd, 2, ATT_HEADS, T), -jnp.inf, f32),
            jnp.zeros((Bd, 2, ATT_HEADS, T), f32),
            jnp.zeros((Bd, 2, ATT_HEADS, T, ATT_DV), f32))
    s_self = jnp.einsum('bqhnd,bkhnd->bnhqk', q, k_new).astype(f32) * scale
    causal = jnp.tril(jnp.ones((T, T), dtype=bool))
    s_self = jnp.where(causal, s_self, -jnp.inf)
    carry = _online_update(init, s_self, v_new)

    def step(carry, ids):
        kp = cache_k[layer, ids].reshape(Bd, PAGE_SIZE, ATT_HEADS, 2, ATT_DH)
        vp = cache_v[layer, ids]
        s = jnp.einsum('bqhnd,bkhnd->bnhqk', q, kp).astype(f32) * scale
        return _online_update(carry, s, vp), None

    (m, l, acc), _ = lax.scan(step, carry, page_table.T)
    o = acc / l[..., None]
    w = o[:, 0] - lam * o[:, 1]
    return jnp.swapaxes(w, 1, 2).astype(v_new.dtype)


def _attn_post(o, g, lam_init):
    o = _rms(o, g) * (1.0 - lam_init)
    return o.reshape(*o.shape[:-2], ATT_W)


def _conv_mixer(c, past, w_dw, b_dw, gn_g, gn_b):
    a, gate = jnp.split(c, 2, axis=-1)
    glu = a * jax.nn.sigmoid(gate)
    buf = jnp.concatenate([past.astype(glu.dtype), glu], axis=1)
    y = lax.conv_general_dilated(
        buf, w_dw[:, None, :].astype(buf.dtype), window_strides=(1,), padding='VALID',
        dimension_numbers=('NWC', 'WIO', 'NWC'), feature_group_count=CONV_W) + b_dw
    lead = y.shape[:-1]
    gs = CONV_W // CONV_GROUPS
    y = _ln(y.reshape(*lead, CONV_GROUPS, gs), gn_g.reshape(CONV_GROUPS, gs),
            gn_b.reshape(CONV_GROUPS, gs)).reshape(*lead, CONV_W)
    return jax.nn.silu(y), buf[:, -(CONV_K - 1):]


def _chunk_mixer(m, ln_g, ln_b, w_s, b_s):
    B, T = m.shape[:2]
    L = min(T, CHUNK)
    z = jax.nn.gelu(m)
    u, v = jnp.split(z, 2, axis=-1)
    u = u.reshape(B, T, GM_HEADS, GM_DH)
    v = _ln(v.reshape(B, T, GM_HEADS, GM_DH), ln_g.reshape(GM_HEADS, GM_DH),
            ln_b.reshape(GM_HEADS, GM_DH))
    W = w_s[:, :L, :L] * jnp.tril(jnp.ones((L, L), w_s.dtype))
    b = jnp.swapaxes(b_s[:, :L], 0, 1)[:, :, None]
    vc = v.reshape(B, T // L, L, GM_HEADS, GM_DH)
    s = jnp.einsum('hij,bcjhd->bcihd', W, vc) + b
    out = u * s.reshape(B, T, GM_HEADS, GM_DH)
    return out.reshape(B, T, GM_W), v.reshape(B, T, GM_W)


def _layer_tail(x, att, conv, gm, w_out, ffn_g, w_gate, w_up, w_down):
    x = x + jnp.concatenate([att, conv, gm], axis=-1) @ w_out
    h = _rms(x, ffn_g)
    return x + (jax.nn.silu(h @ w_gate) * (h @ w_up)) @ w_down


def setup_inputs(seed: int = 0) -> dict:
    key = jax.random.key(seed)
    ks = jax.random.split(key, 32)
    f32 = jnp.float32
    n_pages = PAST_LEN // PAGE_SIZE
    n_used = DEC_BATCH * n_pages
    n_phys = n_used + n_used // 4

    def nrm(k, shape, scale):
        return jax.random.normal(k, shape, f32) * scale

    def gain(k, shape):
        return 1.0 + 0.05 * jax.random.normal(k, shape, f32)

    page_table = jax.random.permutation(ks[5], n_phys)[:n_used].reshape(
        DEC_BATCH, n_pages).astype(jnp.int32)
    return {
        "x_prompt": nrm(ks[0], (BATCH, SEQ, D_MODEL), 1.0),
        "x_sample": nrm(ks[1], (DEC_BATCH, DEC_SEQ, D_MODEL), 1.0),
        "cache_k": nrm(ks[2], (DEPTH, n_phys, PAGE_SIZE, ATT_HEADS, 2 * ATT_DH), 1.0),
        "cache_v": nrm(ks[3], (DEPTH, n_phys, PAGE_SIZE, ATT_HEADS, ATT_DV), 1.0),
        "state_conv": nrm(ks[4], (DEPTH, DEC_BATCH, CONV_K - 1, CONV_W), 0.5),
        "page_table": page_table,
        "mix_norm_g": gain(ks[6], (DEPTH, D_MODEL)),
        "w_in": nrm(ks[7], (DEPTH, D_MODEL, IN_W), D_MODEL ** -0.5),
        "q_norm_g": gain(ks[8], (DEPTH, ATT_DH)),
        "k_norm_g": gain(ks[9], (DEPTH, ATT_DH)),
        "lam_q1": nrm(ks[10], (DEPTH, ATT_DH), 0.1),
        "lam_k1": nrm(ks[11], (DEPTH, ATT_DH), 0.1),
        "lam_q2": nrm(ks[12], (DEPTH, ATT_DH), 0.1),
        "lam_k2": nrm(ks[13], (DEPTH, ATT_DH), 0.1),
        "subln_g": gain(ks[14], (DEPTH, ATT_DV)),
        "conv_w": nrm(ks[15], (DEPTH, CONV_K, CONV_W), CONV_K ** -0.5),
        "conv_b": nrm(ks[16], (DEPTH, CONV_W), 0.02),
        "conv_norm_g": gain(ks[17], (DEPTH, CONV_W)),
        "conv_norm_b": nrm(ks[18], (DEPTH, CONV_W), 0.02),
        "gm_norm_g": gain(ks[19], (DEPTH, GM_W)),
        "gm_norm_b": nrm(ks[20], (DEPTH, GM_W), 0.02),
        "gm_ws": nrm(ks[21], (DEPTH, GM_HEADS, CHUNK, CHUNK), CHUNK ** -0.5),
        "gm_bs": 1.0 + nrm(ks[22], (DEPTH, GM_HEADS, CHUNK), 0.1),
        "w_out": nrm(ks[23], (DEPTH, MIX_W, D_MODEL), MIX_W ** -0.5),
        "ffn_norm_g": gain(ks[24], (DEPTH, D_MODEL)),
        "w_gate": nrm(ks[25], (DEPTH, D_MODEL, D_FF), D_MODEL ** -0.5),
        "w_up": nrm(ks[26], (DEPTH, D_MODEL, D_FF), D_MODEL ** -0.5),
        "w_down": nrm(ks[27], (DEPTH, D_FF, D_MODEL), D_FF ** -0.5),
    }


def reference(x_prompt, x_sample, cache_k, cache_v, state_conv, page_table,
              mix_norm_g, w_in, q_norm_g, k_norm_g, lam_q1, lam_k1, lam_q2, lam_k2,
              subln_g, conv_w, conv_b, conv_norm_g, conv_norm_b, gm_norm_g, gm_norm_b,
              gm_ws, gm_bs, w_out, ffn_norm_g, w_gate, w_up, w_down):
    xp, xs = x_prompt, x_sample
    B, S = xp.shape[:2]
    Bd, T = xs.shape[:2]
    k_p, v_p, c_p, k_s, v_s, c_s, g_s = [], [], [], [], [], [], []
    for l in range(DEPTH):
        lam_init = 0.8 - 0.6 * math.exp(-0.3 * l)
        lam = _diff_lambda(lam_q1[l], lam_k1[l], lam_q2[l], lam_k2[l], lam_init)

        q, k, v, c, m = _in_proj(xp, mix_norm_g[l], w_in[l], q_norm_g[l], k_norm_g[l])
        att = _attn_post(_attn_prompt(q, k, v, lam), subln_g[l], lam_init)
        conv, cst = _conv_mixer(c, jnp.zeros((B, CONV_K - 1, CONV_W), c.dtype),
                                conv_w[l], conv_b[l], conv_norm_g[l], conv_norm_b[l])
        gm, _ = _chunk_mixer(m, gm_norm_g[l], gm_norm_b[l], gm_ws[l], gm_bs[l])
        xp = _layer_tail(xp, att, conv, gm, w_out[l], ffn_norm_g[l],
                         w_gate[l], w_up[l], w_down[l])
        k_p.append(k.reshape(B, S, ATT_HEADS, 2 * ATT_DH))
        v_p.append(v)
        c_p.append(cst)

        q, k, v, c, m = _in_proj(xs, mix_norm_g[l], w_in[l], q_norm_g[l], k_norm_g[l])
        att = _attn_post(_attn_sample(q, k, v, cache_k, cache_v, l, page_table, lam),
                         subln_g[l], lam_init)
        conv, cst = _conv_mixer(c, state_conv[l], conv_w[l], conv_b[l],
                                conv_norm_g[l], conv_norm_b[l])
        gm, vrows = _chunk_mixer(m, gm_norm_g[l], gm_norm_b[l], gm_ws[l], gm_bs[l])
        xs = _layer_tail(xs, att, conv, gm, w_out[l], ffn_norm_g[l],
                         w_gate[l], w_up[l], w_down[l])
        k_s.append(k.reshape(Bd, T, ATT_HEADS, 2 * ATT_DH))
        v_s.append(v)
        c_s.append(cst)
        g_s.append(vrows)

    return (xp, xs, jnp.stack(k_p), jnp.stack(v_p), jnp.stack(c_p),
            jnp.stack(k_s), jnp.stack(v_s), jnp.stack(c_s), jnp.stack(g_s))
```

```python
import math
from functools import partial

import jax
import jax.numpy as jnp
from jax import lax
from jax.experimental import pallas as pl
from jax.experimental.pallas import tpu as pltpu

F32 = jnp.float32
BF16 = jnp.bfloat16

EPS = 1e-6
ATT_HEADS = 4
ATT_DH = 64
ATT_DV = 2 * ATT_DH
ATT_W = ATT_HEADS * ATT_DV
QK_W = ATT_HEADS * 2 * ATT_DH
CONV_W = 256
CONV_K = 31
GM_HEADS = 4
GM_DH = 64
GM_W = GM_HEADS * GM_DH
CHUNK = 128
GROUP = 64
GROUP_TILE = 256
HALO = 32

_Q0, _K0, _V0, _C0, _M0, _END = 0, QK_W, 2 * QK_W, 2 * QK_W + ATT_W, 2 * QK_W + ATT_W + 2 * CONV_W, \
    2 * QK_W + ATT_W + 2 * CONV_W + 2 * GM_W

V7X_VMEM_LIMIT = 56 * 1024 * 1024


def _iota_div(shape, axis, div):
    assert div & (div - 1) == 0
    return lax.broadcasted_iota(jnp.int32, shape, axis) >> (div.bit_length() - 1)


def _iota_mod(shape, axis, mod):
    assert mod & (mod - 1) == 0
    return lax.broadcasted_iota(jnp.int32, shape, axis) & (mod - 1)


def _group_matrix():
    r = _iota_div((GROUP_TILE, GROUP_TILE), 0, GROUP)
    c = _iota_div((GROUP_TILE, GROUP_TILE), 1, GROUP)
    return jnp.where(r == c, 1.0, 0.0).astype(BF16)


def _group_sum(x, gmat):
    return jnp.dot(x.astype(BF16), gmat, preferred_element_type=F32)


def _group_sum_split(x, gmat):
    hi = x.astype(BF16)
    lo = (x - hi.astype(F32)).astype(BF16)
    return (jnp.dot(hi, gmat, preferred_element_type=F32)
            + jnp.dot(lo, gmat, preferred_element_type=F32))


def _rms_rows(x, g):
    return x * lax.rsqrt(jnp.mean(x * x, axis=-1, keepdims=True) + EPS) * g


def _group_rms(z, g, gmat):
    parts = []
    for c in range(0, z.shape[-1], GROUP_TILE):
        zc = z[:, c:c + GROUP_TILE]
        ms = _group_sum(zc * zc, gmat) * (1.0 / GROUP)
        parts.append(zc * lax.rsqrt(ms + EPS))
    y = parts[0] if len(parts) == 1 else jnp.concatenate(parts, axis=-1)
    return y * g


def _group_ln(z, g, b, gmat):
    parts = []
    for c in range(0, z.shape[-1], GROUP_TILE):
        zc = z[:, c:c + GROUP_TILE]
        d = zc - _group_sum_split(zc, gmat) * (1.0 / GROUP)
        var = _group_sum(d * d, gmat) * (1.0 / GROUP)
        parts.append(d * lax.rsqrt(var + EPS))
    y = parts[0] if len(parts) == 1 else jnp.concatenate(parts, axis=-1)
    return y * g + b


def _sigmoid(x):
    return 1.0 / (1.0 + jnp.exp(-x))


def _silu(x):
    return x * _sigmoid(x)


def _gelu_tanh(x):
    return 0.5 * x * (1.0 + jnp.tanh(math.sqrt(2.0 / math.pi) * (x + 0.044715 * (x * x * x))))


def _project(xn, w_ref, lo, hi):
    return jnp.dot(xn, w_ref[:, lo:hi], preferred_element_type=F32)


def _diff_lambda(lq1, lk1, lq2, lk2, lam_init):
    a = jnp.sum(lq1 * lk1, axis=-1, keepdims=True)
    b = jnp.sum(lq2 * lk2, axis=-1, keepdims=True)
    return jnp.exp(a) - jnp.exp(b) + lam_init


def _prompt_mix_kernel(x_ref, ng_ref, w_ref, qg_ref, kg_ref, cw_ref, cb_ref, cng_ref, cnb_ref,
                       gng_ref, gnb_ref, ws_ref, bs_ref,
                       q_out, k_out, kb_out, v_out, vb_out, conv_out, gm_out, cst_out,
                       glu_buf, *, rows_per_conv_chunk):
    j = pl.program_id(1)
    nj = pl.num_programs(1)
    ts = x_ref.shape[1]
    gmat = _group_matrix()

    @pl.when(j == 0)
    def _():
        glu_buf[0:HALO, :] = jnp.zeros((HALO, CONV_W), F32)

    xn = _rms_rows(x_ref[0], ng_ref[...]).astype(BF16)

    q = _group_rms(_project(xn, w_ref, _Q0, _K0), qg_ref[...], gmat) * (ATT_DH ** -0.5)
    q_out[0] = q.astype(BF16)
    k = _group_rms(_project(xn, w_ref, _K0, _V0), kg_ref[...], gmat)
    k_out[0] = k
    kb_out[0] = k.astype(BF16)
    v = _project(xn, w_ref, _V0, _C0)
    v_out[0] = v
    vb_out[0] = v.astype(BF16)

    c = _project(xn, w_ref, _C0, _M0)
    glu_buf[HALO:HALO + ts, :] = c[:, :CONV_W] * _sigmoid(c[:, CONV_W:])
    first = HALO - (CONV_K - 1)
    rc = rows_per_conv_chunk
    ys = []
    for r in range(0, ts, rc):
        acc = jnp.broadcast_to(cb_ref[...], (rc, CONV_W))
        for t in range(CONV_K):
            acc = acc + cw_ref[t:t + 1, :] * glu_buf[r + first + t:r + first + t + rc, :]
        ys.append(acc)
    y = jnp.concatenate(ys, axis=0)
    conv_out[0] = _silu(_group_ln(y, cng_ref[...], cnb_ref[...], gmat)).astype(BF16)

    @pl.when(j == nj - 1)
    def _():
        cst_out[0] = glu_buf[ts + first:ts + HALO, :]

    glu_buf[0:HALO, :] = glu_buf[ts:ts + HALO, :]

    m = _gelu_tanh(_project(xn, w_ref, _M0, _END))
    u = m[:, :GM_W]
    vln = _group_ln(m[:, GM_W:], gng_ref[...], gnb_ref[...], gmat).astype(BF16)
    wr = _iota_mod((GM_HEADS * CHUNK, CHUNK), 0, CHUNK)
    wc = lax.broadcasted_iota(jnp.int32, (GM_HEADS * CHUNK, CHUNK), 1)
    w_tril = jnp.where(wc <= wr, ws_ref[...], 0.0).astype(BF16)
    lane_head = _iota_div((CHUNK, GM_W), 1, GM_DH)
    outs = []
    for r in range(0, ts, CHUNK):
        sv = jnp.dot(w_tril, vln[r:r + CHUNK, :], preferred_element_type=F32)
        s = sv[0:CHUNK, :]
        for h in range(1, GM_HEADS):
            s = jnp.where(lane_head == h, sv[h * CHUNK:(h + 1) * CHUNK, :], s)
        outs.append(u[r:r + CHUNK, :] * (s + bs_ref[...]))
    gm_out[0] = jnp.concatenate(outs, axis=0).astype(BF16)


def _const_spec(shape):
    nd = len(shape)
    return pl.BlockSpec(shape, lambda *_: (0,) * nd, pipeline_mode=pl.Buffered(1))


def _prompt_mix(x, ng, w_in, qg, kg, cw, cb, cng, cnb, gng, gnb, ws, bs, *, ts):
    B, S, D = x.shape
    grid = (B, S // ts)
    row = lambda w: pl.BlockSpec((1, ts, w), lambda b, j: (b, j, 0))
    outs = (
        jax.ShapeDtypeStruct((B, S, QK_W), BF16),
        jax.ShapeDtypeStruct((B, S, QK_W), F32),
        jax.ShapeDtypeStruct((B, S, QK_W), BF16),
        jax.ShapeDtypeStruct((B, S, ATT_W), F32),
        jax.ShapeDtypeStruct((B, S, ATT_W), BF16),
        jax.ShapeDtypeStruct((B, S, CONV_W), BF16),
        jax.ShapeDtypeStruct((B, S, GM_W), BF16),
        jax.ShapeDtypeStruct((B, CONV_K - 1, CONV_W), F32),
    )
    return pl.pallas_call(
        partial(_prompt_mix_kernel, rows_per_conv_chunk=64),
        grid=grid,
        in_specs=[row(D)] + [_const_spec(a.shape) for a in (ng, w_in, qg, kg, cw, cb, cng, cnb, gng, gnb, ws, bs)],
        out_specs=(row(QK_W), row(QK_W), row(QK_W), row(ATT_W), row(ATT_W), row(CONV_W), row(GM_W),
                   pl.BlockSpec((1, CONV_K - 1, CONV_W), lambda b, j: (b, 0, 0))),
        out_shape=outs,
        scratch_shapes=[pltpu.VMEM((HALO + ts, CONV_W), F32)],
        compiler_params=pltpu.CompilerParams(
            dimension_semantics=("arbitrary", "arbitrary"), vmem_limit_bytes=V7X_VMEM_LIMIT),
        name="prompt_mix",
    )(x, ng, w_in, qg, kg, cw, cb, cng, cnb, gng, gnb, ws, bs)


def _sample_mix_kernel(x_ref, st_ref, ng_ref, w_ref, qg_ref, kg_ref, cw_ref, cb_ref, cng_ref, cnb_ref,
                       gng_ref, gnb_ref, wl_ref, bl_ref,
                       q_out, k_out, v_out, conv_out, gm_out, cst_out, vrow_out, buf, *, n_tok, n_batch):
    rows = n_tok * n_batch
    hist = (CONV_K - 1) * n_batch
    gmat = _group_matrix()
    xn = _rms_rows(x_ref[...], ng_ref[...]).astype(BF16)

    q_out[...] = _group_rms(_project(xn, w_ref, _Q0, _K0), qg_ref[...], gmat) * (ATT_DH ** -0.5)
    k_out[...] = _group_rms(_project(xn, w_ref, _K0, _V0), kg_ref[...], gmat)
    v_out[...] = _project(xn, w_ref, _V0, _C0)

    c = _project(xn, w_ref, _C0, _M0)
    buf[0:hist, :] = st_ref[...]
    buf[hist:hist + rows, :] = c[:, :CONV_W] * _sigmoid(c[:, CONV_W:])
    acc = jnp.broadcast_to(cb_ref[...], (rows, CONV_W))
    for t in range(CONV_K):
        acc = acc + cw_ref[t:t + 1, :] * buf[t * n_batch:t * n_batch + rows, :]
    conv_out[...] = _silu(_group_ln(acc, cng_ref[...], cnb_ref[...], gmat)).astype(BF16)
    cst_out[...] = buf[rows:rows + hist, :]

    m = _gelu_tanh(_project(xn, w_ref, _M0, _END))
    u = m[:, :GM_W]
    vln = _group_ln(m[:, GM_W:], gng_ref[...], gnb_ref[...], gmat)
    vrow_out[...] = vln
    outs = []
    for i in range(n_tok):
        s = jnp.broadcast_to(bl_ref[i:i + 1, :], (n_batch, GM_W))
        for jj in range(i + 1):
            s = s + wl_ref[i * n_tok + jj:i * n_tok + jj + 1, :] * vln[jj * n_batch:(jj + 1) * n_batch, :]
        outs.append(u[i * n_batch:(i + 1) * n_batch, :] * s)
    gm_out[...] = jnp.concatenate(outs, axis=0).astype(BF16)


def _sample_mix(x, st, ng, w_in, qg, kg, cw, cb, cng, cnb, gng, gnb, wl, bl, *, n_tok, n_batch):
    rows = n_tok * n_batch
    hist = (CONV_K - 1) * n_batch
    outs = (
        jax.ShapeDtypeStruct((rows, QK_W), F32),
        jax.ShapeDtypeStruct((rows, QK_W), F32),
        jax.ShapeDtypeStruct((rows, ATT_W), F32),
        jax.ShapeDtypeStruct((rows, CONV_W), BF16),
        jax.ShapeDtypeStruct((rows, GM_W), BF16),
        jax.ShapeDtypeStruct((hist, CONV_W), F32),
        jax.ShapeDtypeStruct((rows, GM_W), F32),
    )
    return pl.pallas_call(
        partial(_sample_mix_kernel, n_tok=n_tok, n_batch=n_batch),
        out_shape=outs,
        scratch_shapes=[pltpu.VMEM((hist + rows, CONV_W), F32)],
        compiler_params=pltpu.CompilerParams(vmem_limit_bytes=V7X_VMEM_LIMIT),
        name="sample_mix",
    )(x, st, ng, w_in, qg, kg, cw, cb, cng, cnb, gng, gnb, wl, bl)


def _prompt_attn_kernel(q_ref, k_ref, v_ref, lq1_ref, lk1_ref, lq2_ref, lk2_ref, sg_ref, o_ref,
                        m_scr, l_scr, acc_scr, *, lam_init):
    i = pl.program_id(2)
    tq = q_ref.shape[1]
    tk = tq
    lane = lax.broadcasted_iota(jnp.int32, (tq, ATT_DV), 1)
    q = q_ref[0]
    zero = jnp.zeros_like(q)
    q2 = jnp.concatenate([jnp.where(lane < ATT_DH, q, zero), jnp.where(lane >= ATT_DH, q, zero)], axis=0)

    m_scr[...] = jnp.full(m_scr.shape, -jnp.inf, F32)
    l_scr[...] = jnp.zeros(l_scr.shape, F32)
    acc_scr[...] = jnp.zeros(acc_scr.shape, F32)

    def step(j, masked):
        off = pl.multiple_of(j * tk, tk)
        kb = k_ref[0, pl.ds(off, tk), :]
        vb = v_ref[0, pl.ds(off, tk), :]
        s = lax.dot_general(q2, kb, (((1,), (1,)), ((), ())), preferred_element_type=F32)
        if masked:
            rpos = _iota_mod((2 * tq, tk), 0, tq)
            cpos = lax.broadcasted_iota(jnp.int32, (2 * tq, tk), 1)
            s = jnp.where(cpos <= rpos, s, -jnp.inf)
        m_prev = m_scr[...]
        m_new = jnp.maximum(m_prev, jnp.max(s, axis=-1, keepdims=True))
        alpha = jnp.exp(m_prev - m_new)
        p = jnp.exp(s - m_new)
        l_scr[...] = alpha * l_scr[...] + jnp.sum(p, axis=-1, keepdims=True)
        acc_scr[...] = alpha * acc_scr[...] + jnp.dot(p.astype(BF16), vb, preferred_element_type=F32)
        m_scr[...] = m_new

    def body(j, carry):
        step(j, False)
        return carry

    lax.fori_loop(0, i, body, 0)
    step(i, True)

    lam = _diff_lambda(lq1_ref[...], lk1_ref[...], lq2_ref[...], lk2_ref[...], lam_init)
    o = acc_scr[...] / l_scr[...]
    w = o[:tq] - lam * o[tq:]
    o_ref[0] = (_rms_rows(w, sg_ref[...]) * (1.0 - lam_init)).astype(BF16)


def _prompt_attn(q, kb, vb, lq1, lk1, lq2, lk2, sg, *, lam_init, tq):
    B, S, _ = q.shape
    grid = (B, ATT_HEADS, S // tq)
    vec = lambda a: pl.BlockSpec(a.shape, lambda b, h, i: (0, 0))
    return pl.pallas_call(
        partial(_prompt_attn_kernel, lam_init=lam_init),
        grid=grid,
        in_specs=[pl.BlockSpec((1, tq, ATT_DV), lambda b, h, i: (b, i, h)),
                  pl.BlockSpec((1, S, ATT_DV), lambda b, h, i: (b, 0, h)),
                  pl.BlockSpec((1, S, ATT_DV), lambda b, h, i: (b, 0, h)),
                  vec(lq1), vec(lk1), vec(lq2), vec(lk2), vec(sg)],
        out_specs=pl.BlockSpec((1, tq, ATT_DV), lambda b, h, i: (b, i, h)),
        out_shape=jax.ShapeDtypeStruct((B, S, ATT_W), BF16),
        scratch_shapes=[pltpu.VMEM((2 * tq, 1), F32), pltpu.VMEM((2 * tq, 1), F32),
                        pltpu.VMEM((2 * tq, ATT_DV), F32)],
        compiler_params=pltpu.CompilerParams(
            dimension_semantics=("arbitrary", "arbitrary", "arbitrary"), vmem_limit_bytes=V7X_VMEM_LIMIT),
        name="prompt_attn",
    )(q, kb, vb, lq1, lk1, lq2, lk2, sg)


def _sample_attn_kernel(pt_ref, qbd_ref, kn_ref, vn_ref, lq1_ref, lk1_ref, lq2_ref, lk2_ref, sg_ref, *rest,
                        lam_init, pages_per_step, n_tok):
    g = pages_per_step
    k_refs, v_refs = rest[:g], rest[g:2 * g]
    o_ref, m_scr, l_scr, acc_scr = rest[2 * g:]
    p_idx = pl.program_id(1)
    rows = qbd_ref.shape[1]
    half = rows // 2
    qbd = qbd_ref[0]

    @pl.when(p_idx == 0)
    def _():
        qf = qbd.astype(F32)
        tok = _iota_mod((rows, 1), 0, half) >> (ATT_HEADS.bit_length() - 1)
        ss = []
        for j in range(n_tok):
            sj = jnp.sum(qf * kn_ref[0, j:j + 1, :], axis=-1, keepdims=True)
            ss.append(jnp.where(tok >= j, sj, -jnp.inf))
        m0 = ss[0]
        for sj in ss[1:]:
            m0 = jnp.maximum(m0, sj)
        l0 = jnp.zeros((rows, 1), F32)
        a0 = jnp.zeros((rows, ATT_W), F32)
        for j in range(n_tok):
            pj = jnp.exp(ss[j] - m0)
            l0 = l0 + pj
            a0 = a0 + pj * vn_ref[0, j:j + 1, :]
        m_scr[...] = m0
        l_scr[...] = l0
        acc_scr[...] = a0

    ss = [lax.dot_general(qbd, k_refs[i][...].astype(BF16), (((1,), (1,)), ((), ())),
                          preferred_element_type=F32) for i in range(g)]
    m_prev = m_scr[...]
    m_new = m_prev
    for s in ss:
        m_new = jnp.maximum(m_new, jnp.max(s, axis=-1, keepdims=True))
    alpha = jnp.exp(m_prev - m_new)
    l_new = alpha * l_scr[...]
    acc = alpha * acc_scr[...]
    for i in range(g):
        p = jnp.exp(ss[i] - m_new)
        l_new = l_new + jnp.sum(p, axis=-1, keepdims=True)
        acc = acc + jnp.dot(p.astype(BF16), v_refs[i][...].astype(BF16), preferred_element_type=F32)
    m_scr[...] = m_new
    l_scr[...] = l_new
    acc_scr[...] = acc

    @pl.when(p_idx == pl.num_programs(1) - 1)
    def _():
        head = _iota_mod((rows, ATT_W), 0, ATT_HEADS)
        col_head = _iota_div((rows, ATT_W), 1, ATT_DV)
        own = jnp.where(head == col_head, acc_scr[...], 0.0)
        o = own[:, 0:ATT_DV]
        for h in range(1, ATT_HEADS):
            o = o + own[:, h * ATT_DV:(h + 1) * ATT_DV]
        o = o / l_scr[...]
        lam = _diff_lambda(lq1_ref[...], lk1_ref[...], lq2_ref[...], lk2_ref[...], lam_init)
        w = o[:half] - lam * o[half:]
        o_ref[0] = _rms_rows(w, sg_ref[...]) * (1.0 - lam_init)


def _sample_attn(page_table, qbd, kn, vn, lq1, lk1, lq2, lk2, sg, cache_k, cache_v, *,
                 layer, lam_init, pages_per_step, n_tok):
    Bd, rows, _ = qbd.shape
    n_pages = page_table.shape[1]
    page = cache_k.shape[2]
    g = pages_per_step
    grid = (Bd, n_pages // g)
    pt_flat = page_table.reshape(-1)
    vec = lambda a: pl.BlockSpec(a.shape, lambda b, p, pt: (0, 0))
    per_b = lambda a: pl.BlockSpec((1,) + a.shape[1:], lambda b, p, pt: (b, 0, 0))

    def page_spec(i, width):
        return pl.BlockSpec((None, None, page, width),
                            lambda b, p, pt: (layer, pt[b * n_pages + p * g + i], 0, 0))

    in_specs = ([per_b(qbd), per_b(kn), per_b(vn), vec(lq1), vec(lk1), vec(lq2), vec(lk2), vec(sg)]
                + [page_spec(i, QK_W) for i in range(g)] + [page_spec(i, ATT_W) for i in range(g)])
    return pl.pallas_call(
        partial(_sample_attn_kernel, lam_init=lam_init, pages_per_step=g, n_tok=n_tok),
        grid_spec=pltpu.PrefetchScalarGridSpec(
            num_scalar_prefetch=1, grid=grid, in_specs=in_specs,
            out_specs=pl.BlockSpec((1, rows // 2, ATT_DV), lambda b, p, pt: (b, 0, 0)),
            scratch_shapes=[pltpu.VMEM((rows, 1), F32), pltpu.VMEM((rows, 1), F32),
                            pltpu.VMEM((rows, ATT_W), F32)]),
        out_shape=jax.ShapeDtypeStruct((Bd, rows // 2, ATT_DV), F32),
        compiler_params=pltpu.CompilerParams(
            dimension_semantics=("arbitrary", "arbitrary"), vmem_limit_bytes=V7X_VMEM_LIMIT),
        name="sample_attn",
    )(pt_flat, qbd, kn, vn, lq1, lk1, lq2, lk2, sg, *([cache_k] * g), *([cache_v] * g))


def _tail_kernel(x_ref, att_ref, conv_ref, gm_ref, wo_ref, fg_ref, wg_ref, wu_ref, wd_ref, y_ref,
                 h_scr, acc_scr):
    mix = (jnp.dot(att_ref[...], wo_ref[0:ATT_W, :], preferred_element_type=F32)
           + jnp.dot(conv_ref[...], wo_ref[ATT_W:ATT_W + CONV_W, :], preferred_element_type=F32)
           + jnp.dot(gm_ref[...], wo_ref[ATT_W + CONV_W:, :], preferred_element_type=F32))
    x1 = x_ref[...] + mix
    h_scr[...] = _rms_rows(x1, fg_ref[...]).astype(BF16)
    acc_scr[...] = x1

    def body(c, carry):
        h = h_scr[...]
        gate = jnp.dot(h, wg_ref[c], preferred_element_type=F32)
        up = jnp.dot(h, wu_ref[c], preferred_element_type=F32)
        act = (_silu(gate) * up).astype(BF16)
        acc_scr[...] += jnp.dot(act, wd_ref[c], preferred_element_type=F32)
        return carry

    lax.fori_loop(0, wg_ref.shape[0], body, 0)
    y_ref[...] = acc_scr[...]


def _tail(x, att, conv, gm, wo, fg, wg, wu, wd, *, tm):
    n, d = x.shape
    row = lambda w: pl.BlockSpec((tm, w), lambda i: (i, 0))
    return pl.pallas_call(
        _tail_kernel,
        grid=(n // tm,),
        in_specs=[row(d), row(ATT_W), row(CONV_W), row(GM_W)]
                 + [_const_spec(a.shape) for a in (wo, fg, wg, wu, wd)],
        out_specs=row(d),
        out_shape=jax.ShapeDtypeStruct((n, d), F32),
        scratch_shapes=[pltpu.VMEM((tm, d), BF16), pltpu.VMEM((tm, d), F32)],
        compiler_params=pltpu.CompilerParams(
            dimension_semantics=("arbitrary",), vmem_limit_bytes=V7X_VMEM_LIMIT),
        name="tail",
    )(x, att, conv, gm, wo, fg, wg, wu, wd)


FF_CHUNK = 256


def kernel(x_prompt, x_sample, cache_k, cache_v, state_conv, page_table, mix_norm_g, w_in, q_norm_g, k_norm_g,
           lam_q1, lam_k1, lam_q2, lam_k2, subln_g, conv_w, conv_b, conv_norm_g, conv_norm_b, gm_norm_g,
           gm_norm_b, gm_ws, gm_bs, w_out, ffn_norm_g, w_gate, w_up, w_down):
    B, S, D = x_prompt.shape
    Bd, T, _ = x_sample.shape
    depth = w_in.shape[0]
    d_ff = w_gate.shape[-1]
    n_phys, page = cache_k.shape[1], cache_k.shape[2]
    assert cache_k.shape[3:] == (ATT_HEADS, 2 * ATT_DH) and cache_v.shape[3:] == (ATT_HEADS, ATT_DV)
    assert w_in.shape[-1] == _END and conv_w.shape[1:] == (CONV_K, CONV_W)
    assert gm_ws.shape[1:] == (GM_HEADS, CHUNK, CHUNK) and S % CHUNK == 0 and T <= CHUNK
    assert d_ff % FF_CHUNK == 0
    n_ffc = d_ff // FF_CHUNK

    ck = cache_k.reshape(depth, n_phys, page, QK_W)
    cv = cache_v.reshape(depth, n_phys, page, ATT_W)
    row = lambda a: a.reshape(1, -1)

    xp = x_prompt
    xs = jnp.swapaxes(x_sample, 0, 1).reshape(T * Bd, D)
    r_idx = jnp.arange(2 * T * ATT_HEADS)
    r_group = (r_idx % ATT_HEADS) * 2 + r_idx // (T * ATT_HEADS)
    qbd_mask = (jnp.arange(QK_W)[None, :] // ATT_DH == r_group[:, None]).astype(F32)

    k_p, v_p, c_p, k_s, v_s, c_s, g_s = [], [], [], [], [], [], []
    for l in range(depth):
        lam_init = 0.8 - 0.6 * math.exp(-0.3 * l)
        w_in_b = w_in[l].astype(BF16)
        wo_b = w_out[l].astype(BF16)
        wg_b = jnp.swapaxes(w_gate[l].reshape(D, n_ffc, FF_CHUNK), 0, 1).astype(BF16)
        wu_b = jnp.swapaxes(w_up[l].reshape(D, n_ffc, FF_CHUNK), 0, 1).astype(BF16)
        wd_b = w_down[l].reshape(n_ffc, FF_CHUNK, D).astype(BF16)
        qg = row(jnp.tile(q_norm_g[l], QK_W // ATT_DH))
        kg = row(jnp.tile(k_norm_g[l], QK_W // ATT_DH))
        lam_vecs = (row(lam_q1[l]), row(lam_k1[l]), row(lam_q2[l]), row(lam_k2[l]))
        sg = row(subln_g[l])
        mix_args = (row(mix_norm_g[l]), w_in_b, qg, kg, conv_w[l], row(conv_b[l]), row(conv_norm_g[l]),
                    row(conv_norm_b[l]), row(gm_norm_g[l]), row(gm_norm_b[l]))
        tail_w = (wo_b, row(ffn_norm_g[l]), wg_b, wu_b, wd_b)

        ws_stack = gm_ws[l].reshape(GM_HEADS * CHUNK, CHUNK)
        bs_rows = jnp.repeat(gm_bs[l].T, GM_DH, axis=1)
        q, k, kb, v, vb, conv, gm, cst = _prompt_mix(xp, *mix_args, ws_stack, bs_rows, ts=512)
        att = _prompt_attn(q, kb, vb, *lam_vecs, sg, lam_init=lam_init, tq=256)
        xp = _tail(xp.reshape(B * S, D), att.reshape(B * S, ATT_W), conv.reshape(B * S, CONV_W),
                   gm.reshape(B * S, GM_W), *tail_w, tm=512).reshape(B, S, D)
        k_p.append(k.reshape(B, S, ATT_HEADS, 2 * ATT_DH))
        v_p.append(v.reshape(B, S, ATT_HEADS, ATT_DV))
        c_p.append(cst)

        st = jnp.swapaxes(state_conv[l], 0, 1).reshape((CONV_K - 1) * Bd, CONV_W)
        wl = jnp.repeat(jnp.transpose(gm_ws[l][:, :T, :T], (1, 2, 0)).reshape(T * T, GM_HEADS), GM_DH, axis=1)
        bl = jnp.repeat(gm_bs[l][:, :T].T, GM_DH, axis=1)
        qs, ks, vs, conv, gm, cst, vrows = _sample_mix(xs, st, *mix_args, wl, bl, n_tok=T, n_batch=Bd)
        tb = lambda a: jnp.swapaxes(a.reshape(T, Bd, -1), 0, 1)
        q_bt, k_bt, v_bt = tb(qs), tb(ks), tb(vs)
        qbd = (jnp.tile(q_bt[:, :, None, :], (1, 2, ATT_HEADS, 1)).reshape(Bd, 2 * T * ATT_HEADS, QK_W)
               * qbd_mask[None]).astype(BF16)
        att = _sample_attn(page_table, qbd, k_bt, v_bt, *lam_vecs, sg, ck, cv, layer=l, lam_init=lam_init,
                           pages_per_step=8, n_tok=T)
        att = jnp.swapaxes(att.reshape(Bd, T, ATT_W), 0, 1).reshape(T * Bd, ATT_W).astype(BF16)
        xs = _tail(xs, att, conv, gm, *tail_w, tm=T * Bd)
        k_s.append(k_bt.reshape(Bd, T, ATT_HEADS, 2 * ATT_DH))
        v_s.append(v_bt.reshape(Bd, T, ATT_HEADS, ATT_DV))
        c_s.append(jnp.swapaxes(cst.reshape(CONV_K - 1, Bd, CONV_W), 0, 1))
        g_s.append(tb(vrows))

    ys = jnp.swapaxes(xs.reshape(T, Bd, D), 0, 1)
    return (xp, ys, jnp.stack(k_p), jnp.stack(v_p), jnp.stack(c_p),
            jnp.stack(k_s), jnp.stack(v_s), jnp.stack(c_s), jnp.stack(g_s))
```

```python
import math
from functools import partial

import jax
import jax.numpy as jnp
from jax import lax
from jax.experimental import pallas as pl
from jax.experimental.pallas import tpu as pltpu

F32 = jnp.float32
BF16 = jnp.bfloat16

EPS = 1e-6
ATT_HEADS = 4
ATT_DH = 64
ATT_DV = 2 * ATT_DH
ATT_W = ATT_HEADS * ATT_DV
QK_W = ATT_HEADS * 2 * ATT_DH
CONV_W = 256
CONV_K = 31
GM_HEADS = 4
GM_DH = 64
GM_W = GM_HEADS * GM_DH
CHUNK = 128
GROUP = 64
GROUP_TILE = 256
HALO = 32
LANES = 128
SUBLANES = 8
FF_CHUNK = 256

_Q0, _K0, _V0, _C0, _M0, _END = 0, QK_W, 2 * QK_W, 2 * QK_W + ATT_W, 2 * QK_W + ATT_W + 2 * CONV_W, \
    2 * QK_W + ATT_W + 2 * CONV_W + 2 * GM_W

Q_SCALE = ATT_DH ** -0.5 * math.log2(math.e)

V7X_VMEM_LIMIT = 56 * 1024 * 1024


def _iota_div(shape, axis, div):
    assert div & (div - 1) == 0
    return lax.broadcasted_iota(jnp.int32, shape, axis) >> (div.bit_length() - 1)


def _iota_mod(shape, axis, mod):
    assert mod & (mod - 1) == 0
    return lax.broadcasted_iota(jnp.int32, shape, axis) & (mod - 1)


def _group_matrix():
    r = _iota_div((GROUP_TILE, GROUP_TILE), 0, GROUP)
    c = _iota_div((GROUP_TILE, GROUP_TILE), 1, GROUP)
    return jnp.where(r == c, 1.0, 0.0).astype(BF16)


def _group_sum(x, gmat):
    return jnp.dot(x.astype(BF16), gmat, preferred_element_type=F32)


def _group_sum_split(x, gmat):
    hi = x.astype(BF16)
    lo = (x - hi.astype(F32)).astype(BF16)
    return (jnp.dot(hi, gmat, preferred_element_type=F32)
            + jnp.dot(lo, gmat, preferred_element_type=F32))


def _rms_rows(x, g):
    return x * lax.rsqrt(jnp.mean(x * x, axis=-1, keepdims=True) + EPS) * g


def _group_rms(z, g, gmat):
    parts = []
    for c in range(0, z.shape[-1], GROUP_TILE):
        zc = z[:, c:c + GROUP_TILE]
        ms = _group_sum(zc * zc, gmat) * (1.0 / GROUP)
        parts.append(zc * lax.rsqrt(ms + EPS))
    y = parts[0] if len(parts) == 1 else jnp.concatenate(parts, axis=-1)
    return y * g


def _group_ln(z, g, b, gmat):
    parts = []
    for c in range(0, z.shape[-1], GROUP_TILE):
        zc = z[:, c:c + GROUP_TILE]
        d = zc - _group_sum_split(zc, gmat) * (1.0 / GROUP)
        var = _group_sum(d * d, gmat) * (1.0 / GROUP)
        parts.append(d * lax.rsqrt(var + EPS))
    y = parts[0] if len(parts) == 1 else jnp.concatenate(parts, axis=-1)
    return y * g + b


def _sigmoid(x):
    return 1.0 / (1.0 + jnp.exp(-x))


def _silu(x):
    return x * _sigmoid(x)


def _gelu_tanh(x):
    return 0.5 * x * (1.0 + jnp.tanh(math.sqrt(2.0 / math.pi) * (x + 0.044715 * (x * x * x))))


def _project(xn, w_ref, lo, hi):
    return jnp.dot(xn, w_ref[:, lo:hi], preferred_element_type=F32)


def _diff_lambda(lq1, lk1, lq2, lk2, lam_init):
    a = jnp.sum(lq1 * lk1, axis=-1, keepdims=True)
    b = jnp.sum(lq2 * lk2, axis=-1, keepdims=True)
    return jnp.exp(a) - jnp.exp(b) + lam_init


def _split_maps(q):
    lane = lax.broadcasted_iota(jnp.int32, q.shape, q.ndim - 1)
    zero = jnp.zeros_like(q)
    return jnp.concatenate([jnp.where(lane < ATT_DH, q, zero), jnp.where(lane >= ATT_DH, q, zero)],
                           axis=q.ndim - 2)


def _nt_dot(a, b):
    return lax.dot_general(a, b, (((1,), (1,)), ((), ())), preferred_element_type=F32)


def _prompt_mix_kernel(x_ref, ng_ref, w_ref, qg_ref, kg_ref, cw_ref, cb_ref, cng_ref, cnb_ref,
                       gng_ref, gnb_ref, ws_ref, bs_ref,
                       q_out, k_out, kb_out, v_out, vb_out, conv_out, gm_out, cst_out,
                       glu_buf, shifted, *, rows_per_conv_chunk):
    j = pl.program_id(1)
    nj = pl.num_programs(1)
    ts = x_ref.shape[1]
    gmat = _group_matrix()

    @pl.when(j == 0)
    def _():
        glu_buf[0:HALO, :] = jnp.zeros((HALO, CONV_W), F32)

    xn = _rms_rows(x_ref[0], ng_ref[...]).astype(BF16)

    q = _group_rms(_project(xn, w_ref, _Q0, _K0), qg_ref[...], gmat) * Q_SCALE
    q_out[0] = q.astype(BF16)
    k = _group_rms(_project(xn, w_ref, _K0, _V0), kg_ref[...], gmat)
    kb_out[0] = k.astype(BF16)
    v = _project(xn, w_ref, _V0, _C0)
    vb_out[0] = v.astype(BF16)
    for h in range(ATT_HEADS):
        k_out[0, pl.ds(h, ts, stride=ATT_HEADS), :] = k[:, h * ATT_DV:(h + 1) * ATT_DV]
        v_out[0, pl.ds(h, ts, stride=ATT_HEADS), :] = v[:, h * ATT_DV:(h + 1) * ATT_DV]

    c = _project(xn, w_ref, _C0, _M0)
    glu_buf[HALO:HALO + ts, :] = c[:, :CONV_W] * _sigmoid(c[:, CONV_W:])
    first = HALO - (CONV_K - 1)
    n_shift = shifted.shape[1]
    for s in range(1, SUBLANES):
        shifted[s - 1] = glu_buf[s:s + n_shift, :]
    rc = rows_per_conv_chunk
    ys = []
    for r in range(0, ts, rc):
        acc = jnp.broadcast_to(cb_ref[...], (rc, CONV_W))
        for t in range(CONV_K):
            base, phase = (first + t) // SUBLANES * SUBLANES, (first + t) % SUBLANES
            src = glu_buf if phase == 0 else shifted.at[phase - 1]
            acc = acc + cw_ref[t:t + 1, :] * src[r + base:r + base + rc, :]
        ys.append(acc)
    y = jnp.concatenate(ys, axis=0)
    conv_out[0] = _silu(_group_ln(y, cng_ref[...], cnb_ref[...], gmat)).astype(BF16)

    @pl.when(j == nj - 1)
    def _():
        cst_out[0] = glu_buf[ts + first:ts + HALO, :]

    glu_buf[0:HALO, :] = glu_buf[ts:ts + HALO, :]

    m = _gelu_tanh(_project(xn, w_ref, _M0, _END))
    u = m[:, :GM_W]
    vln = _group_ln(m[:, GM_W:], gng_ref[...], gnb_ref[...], gmat).astype(BF16)
    wr = _iota_mod((GM_HEADS * CHUNK, CHUNK), 0, CHUNK)
    wc = lax.broadcasted_iota(jnp.int32, (GM_HEADS * CHUNK, CHUNK), 1)
    w_tril = jnp.where(wc <= wr, ws_ref[...], 0.0).astype(BF16)
    lane_head = _iota_div((CHUNK, GM_W), 1, GM_DH)
    outs = []
    for r in range(0, ts, CHUNK):
        sv = jnp.dot(w_tril, vln[r:r + CHUNK, :], preferred_element_type=F32)
        s = sv[0:CHUNK, :]
        for h in range(1, GM_HEADS):
            s = jnp.where(lane_head == h, sv[h * CHUNK:(h + 1) * CHUNK, :], s)
        outs.append(u[r:r + CHUNK, :] * (s + bs_ref[...]))
    gm_out[0] = jnp.concatenate(outs, axis=0).astype(BF16)


def _const_spec(shape):
    nd = len(shape)
    return pl.BlockSpec(shape, lambda *_: (0,) * nd, pipeline_mode=pl.Buffered(1))


def _prompt_mix(x, ng, w_in, qg, kg, cw, cb, cng, cnb, gng, gnb, ws, bs, *, ts):
    B, S, D = x.shape
    grid = (B, S // ts)
    row = lambda w: pl.BlockSpec((1, ts, w), lambda b, j: (b, j, 0))
    by_head = pl.BlockSpec((1, ts * ATT_HEADS, ATT_DV), lambda b, j: (b, j, 0))
    outs = (
        jax.ShapeDtypeStruct((B, S, QK_W), BF16),
        jax.ShapeDtypeStruct((B, S * ATT_HEADS, 2 * ATT_DH), F32),
        jax.ShapeDtypeStruct((B, S, QK_W), BF16),
        jax.ShapeDtypeStruct((B, S * ATT_HEADS, ATT_DV), F32),
        jax.ShapeDtypeStruct((B, S, ATT_W), BF16),
        jax.ShapeDtypeStruct((B, S, CONV_W), BF16),
        jax.ShapeDtypeStruct((B, S, GM_W), BF16),
        jax.ShapeDtypeStruct((B, CONV_K - 1, CONV_W), F32),
    )
    return pl.pallas_call(
        partial(_prompt_mix_kernel, rows_per_conv_chunk=64),
        grid=grid,
        in_specs=[row(D)] + [_const_spec(a.shape) for a in (ng, w_in, qg, kg, cw, cb, cng, cnb, gng, gnb, ws, bs)],
        out_specs=(row(QK_W), by_head, row(QK_W), by_head, row(ATT_W), row(CONV_W), row(GM_W),
                   pl.BlockSpec((1, CONV_K - 1, CONV_W), lambda b, j: (b, 0, 0))),
        out_shape=outs,
        scratch_shapes=[pltpu.VMEM((HALO + ts, CONV_W), F32),
                        pltpu.VMEM((SUBLANES - 1, HALO + ts - SUBLANES, CONV_W), F32)],
        compiler_params=pltpu.CompilerParams(
            dimension_semantics=("arbitrary", "arbitrary"), vmem_limit_bytes=V7X_VMEM_LIMIT),
        name="prompt_mix",
    )(x, ng, w_in, qg, kg, cw, cb, cng, cnb, gng, gnb, ws, bs)


def _sample_mix_kernel(x_ref, st_ref, ng_ref, w_ref, qg_ref, kg_ref, cw_ref, cb_ref, cng_ref, cnb_ref,
                       gng_ref, gnb_ref, wl_ref, bl_ref,
                       q_out, k_out, v_out, conv_out, gm_out, cst_out, vrow_out, buf, *, n_tok, n_batch):
    rows = n_tok * n_batch
    hist = (CONV_K - 1) * n_batch
    gmat = _group_matrix()
    xn = _rms_rows(x_ref[...], ng_ref[...]).astype(BF16)

    q_out[...] = _group_rms(_project(xn, w_ref, _Q0, _K0), qg_ref[...], gmat) * Q_SCALE
    k_out[...] = _group_rms(_project(xn, w_ref, _K0, _V0), kg_ref[...], gmat)
    v_out[...] = _project(xn, w_ref, _V0, _C0)

    c = _project(xn, w_ref, _C0, _M0)
    buf[0:hist, :] = st_ref[...]
    buf[hist:hist + rows, :] = c[:, :CONV_W] * _sigmoid(c[:, CONV_W:])
    acc = jnp.broadcast_to(cb_ref[...], (rows, CONV_W))
    for t in range(CONV_K):
        acc = acc + cw_ref[t:t + 1, :] * buf[t * n_batch:t * n_batch + rows, :]
    conv_out[...] = _silu(_group_ln(acc, cng_ref[...], cnb_ref[...], gmat)).astype(BF16)
    cst_out[...] = buf[rows:rows + hist, :]

    m = _gelu_tanh(_project(xn, w_ref, _M0, _END))
    u = m[:, :GM_W]
    vln = _group_ln(m[:, GM_W:], gng_ref[...], gnb_ref[...], gmat)
    vrow_out[...] = vln
    outs = []
    for i in range(n_tok):
        s = jnp.broadcast_to(bl_ref[i:i + 1, :], (n_batch, GM_W))
        for jj in range(i + 1):
            s = s + wl_ref[i * n_tok + jj:i * n_tok + jj + 1, :] * vln[jj * n_batch:(jj + 1) * n_batch, :]
        outs.append(u[i * n_batch:(i + 1) * n_batch, :] * s)
    gm_out[...] = jnp.concatenate(outs, axis=0).astype(BF16)


def _sample_mix(x, st, ng, w_in, qg, kg, cw, cb, cng, cnb, gng, gnb, wl, bl, *, n_tok, n_batch):
    rows = n_tok * n_batch
    hist = (CONV_K - 1) * n_batch
    outs = (
        jax.ShapeDtypeStruct((rows, QK_W), F32),
        jax.ShapeDtypeStruct((rows, QK_W), F32),
        jax.ShapeDtypeStruct((rows, ATT_W), F32),
        jax.ShapeDtypeStruct((rows, CONV_W), BF16),
        jax.ShapeDtypeStruct((rows, GM_W), BF16),
        jax.ShapeDtypeStruct((hist, CONV_W), F32),
        jax.ShapeDtypeStruct((rows, GM_W), F32),
    )
    return pl.pallas_call(
        partial(_sample_mix_kernel, n_tok=n_tok, n_batch=n_batch),
        out_shape=outs,
        scratch_shapes=[pltpu.VMEM((hist + rows, CONV_W), F32)],
        compiler_params=pltpu.CompilerParams(vmem_limit_bytes=V7X_VMEM_LIMIT),
        name="sample_mix",
    )(x, st, ng, w_in, qg, kg, cw, cb, cng, cnb, gng, gnb, wl, bl)


def _lane_tiles(x):
    return [x[:, t:t + LANES] for t in range(0, x.shape[-1], LANES)]


def _prompt_attn_kernel(q_ref, k_ref, v_ref, lq1_ref, lk1_ref, lq2_ref, lk2_ref, sg_ref, o_ref, o_scr, *,
                        lam_init, chunk):
    i = pl.program_id(2)
    tq = q_ref.shape[1]
    q2 = _split_maps(q_ref[0])
    lam = _diff_lambda(lq1_ref[...], lk1_ref[...], lq2_ref[...], lk2_ref[...], lam_init)

    def attend_chunk(r0, n_kb):
        n_keys = n_kb * chunk
        s = _nt_dot(q2[r0:r0 + chunk], k_ref[0, 0:n_keys, :])
        tiles = _lane_tiles(s)
        n_diag = chunk // LANES
        for t in range(n_diag):
            rpos = lax.broadcasted_iota(jnp.int32, (chunk, LANES), 0)
            cpos = lax.broadcasted_iota(jnp.int32, (chunk, LANES), 1) + t * LANES
            tiles[t - n_diag] = jnp.where(cpos <= rpos, tiles[t - n_diag], -jnp.inf)
        m_tile = tiles[0]
        for t in tiles[1:]:
            m_tile = jnp.maximum(m_tile, t)
        m = jnp.broadcast_to(jnp.max(m_tile, axis=-1, keepdims=True), (chunk, LANES))
        ps = [jnp.exp2(t - m) for t in tiles]
        l_tile = ps[0]
        for p in ps[1:]:
            l_tile = l_tile + p
        acc = jnp.dot(jnp.concatenate(ps, axis=-1).astype(BF16), v_ref[0, 0:n_keys, :],
                      preferred_element_type=F32)
        o_scr[r0:r0 + chunk, :] = acc / jnp.sum(l_tile, axis=-1, keepdims=True)

    def attend(blk):
        for r0 in range(0, 2 * tq, chunk):
            attend_chunk(r0, (blk * tq + r0 % tq) // chunk + 1)
        w = o_scr[0:tq, :] - lam * o_scr[tq:2 * tq, :]
        o_ref[0] = (_rms_rows(w, sg_ref[...]) * (1.0 - lam_init)).astype(BF16)

    for blk in range(k_ref.shape[1] // tq):
        pl.when(i == blk)(partial(attend, blk))


def _prompt_attn(q, kb, vb, lq1, lk1, lq2, lk2, sg, *, lam_init, tq):
    B, S, _ = q.shape
    grid = (B, ATT_HEADS, S // tq)
    vec = lambda a: pl.BlockSpec(a.shape, lambda b, h, i: (0, 0))
    return pl.pallas_call(
        partial(_prompt_attn_kernel, lam_init=lam_init, chunk=256),
        grid=grid,
        in_specs=[pl.BlockSpec((1, tq, ATT_DV), lambda b, h, i: (b, i, h)),
                  pl.BlockSpec((1, S, ATT_DV), lambda b, h, i: (b, 0, h)),
                  pl.BlockSpec((1, S, ATT_DV), lambda b, h, i: (b, 0, h)),
                  vec(lq1), vec(lk1), vec(lq2), vec(lk2), vec(sg)],
        out_specs=pl.BlockSpec((1, tq, ATT_DV), lambda b, h, i: (b, i, h)),
        out_shape=jax.ShapeDtypeStruct((B, S, ATT_W), BF16),
        scratch_shapes=[pltpu.VMEM((2 * tq, ATT_DV), F32)],
        compiler_params=pltpu.CompilerParams(
            dimension_semantics=("arbitrary", "arbitrary", "arbitrary"), vmem_limit_bytes=V7X_VMEM_LIMIT),
        name="prompt_attn",
    )(q, kb, vb, lq1, lk1, lq2, lk2, sg)


def _sample_attn_kernel(pt_ref, qx_ref, kn_ref, vn_ref, lq1_ref, lk1_ref, lq2_ref, lk2_ref, sg_ref, *rest,
                        lam_init, pages_per_step, n_tok):
    g = pages_per_step
    k_refs, v_refs = rest[:g], rest[g:2 * g]
    o_ref, m_scr, l_scr, acc_scr = rest[2 * g:]
    p_idx = pl.program_id(1)
    rows = qx_ref.shape[1]
    half = rows // 2
    page_rows = k_refs[0].shape[0]
    qx = qx_ref[0]

    @pl.when(p_idx == 0)
    def _():
        qf = qx.astype(F32)
        tok = _iota_mod((rows, 1), 0, half) >> (ATT_HEADS.bit_length() - 1)
        ss = []
        for j in range(n_tok):
            sj = jnp.sum(qf * kn_ref[0, j], axis=-1, keepdims=True)
            ss.append(jnp.where(tok >= j, sj, -jnp.inf))
        m0 = ss[0]
        for sj in ss[1:]:
            m0 = jnp.maximum(m0, sj)
        l0 = jnp.zeros((rows, 1), F32)
        a0 = jnp.zeros((rows, ATT_DV), F32)
        for j in range(n_tok):
            pj = jnp.exp2(ss[j] - m0)
            l0 = l0 + pj
            a0 = a0 + pj * vn_ref[0, j]
        m_scr[...] = m0
        l_scr[...] = l0
        acc_scr[...] = a0

    own = _iota_mod((rows, page_rows), 1, ATT_HEADS) == _iota_mod((rows, page_rows), 0, ATT_HEADS)
    ss = [jnp.where(own, _nt_dot(qx, k_refs[i][...].astype(BF16)), -jnp.inf) for i in range(g)]
    m_prev = m_scr[...]
    m_new = m_prev
    for s in ss:
        m_new = jnp.maximum(m_new, jnp.max(s, axis=-1, keepdims=True))
    alpha = jnp.exp2(m_prev - m_new)
    l_new = alpha * l_scr[...]
    acc = alpha * acc_scr[...]
    for i in range(g):
        p = jnp.exp2(ss[i] - m_new)
        l_new = l_new + jnp.sum(p, axis=-1, keepdims=True)
        acc = acc + jnp.dot(p.astype(BF16), v_refs[i][...].astype(BF16), preferred_element_type=F32)
    m_scr[...] = m_new
    l_scr[...] = l_new
    acc_scr[...] = acc

    @pl.when(p_idx == pl.num_programs(1) - 1)
    def _():
        o = acc_scr[...] / l_scr[...]
        lam = _diff_lambda(lq1_ref[...], lk1_ref[...], lq2_ref[...], lk2_ref[...], lam_init)
        w = o[:half] - lam * o[half:]
        o_ref[0] = _rms_rows(w, sg_ref[...]) * (1.0 - lam_init)


def _sample_attn(page_table, qx, kn, vn, lq1, lk1, lq2, lk2, sg, cache_k, cache_v, *,
                 layer, lam_init, pages_per_step, n_tok):
    Bd, rows, _ = qx.shape
    n_pages = page_table.shape[1]
    page_rows = cache_k.shape[2]
    g = pages_per_step
    grid = (Bd, n_pages // g)
    pt_flat = page_table.reshape(-1)
    vec = lambda a: pl.BlockSpec(a.shape, lambda b, p, pt: (0, 0))
    per_b = lambda a: pl.BlockSpec((1,) + a.shape[1:], lambda b, p, pt: (b,) + (0,) * (a.ndim - 1))

    def page_spec(i):
        return pl.BlockSpec((None, None, page_rows, ATT_DV),
                            lambda b, p, pt: (layer, pt[b * n_pages + p * g + i], 0, 0))

    in_specs = ([per_b(qx), per_b(kn), per_b(vn), vec(lq1), vec(lk1), vec(lq2), vec(lk2), vec(sg)]
                + [page_spec(i) for i in range(g)] * 2)
    return pl.pallas_call(
        partial(_sample_attn_kernel, lam_init=lam_init, pages_per_step=g, n_tok=n_tok),
        grid_spec=pltpu.PrefetchScalarGridSpec(
            num_scalar_prefetch=1, grid=grid, in_specs=in_specs,
            out_specs=pl.BlockSpec((1, rows // 2, ATT_DV), lambda b, p, pt: (b, 0, 0)),
            scratch_shapes=[pltpu.VMEM((rows, 1), F32), pltpu.VMEM((rows, 1), F32),
                            pltpu.VMEM((rows, ATT_DV), F32)]),
        out_shape=jax.ShapeDtypeStruct((Bd, rows // 2, ATT_DV), F32),
        compiler_params=pltpu.CompilerParams(
            dimension_semantics=("arbitrary", "arbitrary"), vmem_limit_bytes=V7X_VMEM_LIMIT),
        name="sample_attn",
    )(pt_flat, qx, kn, vn, lq1, lk1, lq2, lk2, sg, *([cache_k] * g), *([cache_v] * g))


def _tail_kernel(x_ref, att_ref, conv_ref, gm_ref, wo_ref, fg_ref, wg_ref, wu_ref, wd_ref, y_ref, act_scr):
    mix = (jnp.dot(att_ref[...], wo_ref[0:ATT_W, :], preferred_element_type=F32)
           + jnp.dot(conv_ref[...], wo_ref[ATT_W:ATT_W + CONV_W, :], preferred_element_type=F32)
           + jnp.dot(gm_ref[...], wo_ref[ATT_W + CONV_W:, :], preferred_element_type=F32))
    x1 = x_ref[...] + mix
    h = _rms_rows(x1, fg_ref[...]).astype(BF16)
    for c in range(0, wg_ref.shape[1], FF_CHUNK):
        gate = jnp.dot(h, wg_ref[:, c:c + FF_CHUNK], preferred_element_type=F32)
        up = jnp.dot(h, wu_ref[:, c:c + FF_CHUNK], preferred_element_type=F32)
        act_scr[:, c:c + FF_CHUNK] = (_silu(gate) * up).astype(BF16)
    y_ref[...] = x1 + jnp.dot(act_scr[...], wd_ref[...], preferred_element_type=F32)


def _tail(x, att, conv, gm, wo, fg, wg, wu, wd, *, tm):
    n, d = x.shape
    d_ff = wg.shape[1]
    assert d_ff % FF_CHUNK == 0
    row = lambda w: pl.BlockSpec((tm, w), lambda i: (i, 0))
    return pl.pallas_call(
        _tail_kernel,
        grid=(n // tm,),
        in_specs=[row(d), row(ATT_W), row(CONV_W), row(GM_W)]
                 + [_const_spec(a.shape) for a in (wo, fg, wg, wu, wd)],
        out_specs=row(d),
        out_shape=jax.ShapeDtypeStruct((n, d), F32),
        scratch_shapes=[pltpu.VMEM((tm, d_ff), BF16)],
        compiler_params=pltpu.CompilerParams(
            dimension_semantics=("arbitrary",), vmem_limit_bytes=V7X_VMEM_LIMIT),
        name="tail",
    )(x, att, conv, gm, wo, fg, wg, wu, wd)


def kernel(x_prompt, x_sample, cache_k, cache_v, state_conv, page_table, mix_norm_g, w_in, q_norm_g, k_norm_g,
           lam_q1, lam_k1, lam_q2, lam_k2, subln_g, conv_w, conv_b, conv_norm_g, conv_norm_b, gm_norm_g,
           gm_norm_b, gm_ws, gm_bs, w_out, ffn_norm_g, w_gate, w_up, w_down):
    B, S, D = x_prompt.shape
    Bd, T, _ = x_sample.shape
    depth = w_in.shape[0]
    n_phys, page = cache_k.shape[1], cache_k.shape[2]
    assert cache_k.shape[3:] == (ATT_HEADS, 2 * ATT_DH) and cache_v.shape[3:] == (ATT_HEADS, ATT_DV)
    assert w_in.shape[-1] == _END and conv_w.shape[1:] == (CONV_K, CONV_W)
    assert gm_ws.shape[1:] == (GM_HEADS, CHUNK, CHUNK) and S % CHUNK == 0 and T <= CHUNK

    ck = cache_k.reshape(depth, n_phys, page * ATT_HEADS, 2 * ATT_DH)
    cv = cache_v.reshape(depth, n_phys, page * ATT_HEADS, ATT_DV)
    row = lambda a: a.reshape(1, -1)

    xp = x_prompt
    xs = jnp.swapaxes(x_sample, 0, 1).reshape(T * Bd, D)

    k_p, v_p, c_p, k_s, v_s, c_s, g_s = [], [], [], [], [], [], []
    for l in range(depth):
        lam_init = 0.8 - 0.6 * math.exp(-0.3 * l)
        w_in_b = w_in[l].astype(BF16)
        qg = row(jnp.tile(q_norm_g[l], QK_W // ATT_DH))
        kg = row(jnp.tile(k_norm_g[l], QK_W // ATT_DH))
        lam_vecs = (row(lam_q1[l]), row(lam_k1[l]), row(lam_q2[l]), row(lam_k2[l]))
        sg = row(subln_g[l])
        mix_args = (row(mix_norm_g[l]), w_in_b, qg, kg, conv_w[l], row(conv_b[l]), row(conv_norm_g[l]),
                    row(conv_norm_b[l]), row(gm_norm_g[l]), row(gm_norm_b[l]))
        tail_w = (w_out[l].astype(BF16), row(ffn_norm_g[l]), w_gate[l].astype(BF16), w_up[l].astype(BF16),
                  w_down[l].astype(BF16))

        ws_stack = gm_ws[l].reshape(GM_HEADS * CHUNK, CHUNK)
        bs_rows = jnp.repeat(gm_bs[l].T, GM_DH, axis=1)
        q, k, kb, v, vb, conv, gm, cst = _prompt_mix(xp, *mix_args, ws_stack, bs_rows, ts=512)
        att = _prompt_attn(q, kb, vb, *lam_vecs, sg, lam_init=lam_init, tq=512)
        xp = _tail(xp.reshape(B * S, D), att.reshape(B * S, ATT_W), conv.reshape(B * S, CONV_W),
                   gm.reshape(B * S, GM_W), *tail_w, tm=512).reshape(B, S, D)
        k_p.append(k.reshape(B, S, ATT_HEADS, 2 * ATT_DH))
        v_p.append(v.reshape(B, S, ATT_HEADS, ATT_DV))
        c_p.append(cst)

        st = jnp.swapaxes(state_conv[l], 0, 1).reshape((CONV_K - 1) * Bd, CONV_W)
        wl = jnp.repeat(jnp.transpose(gm_ws[l][:, :T, :T], (1, 2, 0)).reshape(T * T, GM_HEADS), GM_DH, axis=1)
        bl = jnp.repeat(gm_bs[l][:, :T].T, GM_DH, axis=1)
        qs, ks, vs, conv, gm, cst, vrows = _sample_mix(xs, st, *mix_args, wl, bl, n_tok=T, n_batch=Bd)
        tb = lambda a: jnp.swapaxes(a.reshape(T, Bd, -1), 0, 1)
        k_bt = tb(ks).reshape(Bd, T, ATT_HEADS, 2 * ATT_DH)
        v_bt = tb(vs).reshape(Bd, T, ATT_HEADS, ATT_DV)
        qx = _split_maps(tb(qs).reshape(Bd, T * ATT_HEADS, 2 * ATT_DH)).astype(BF16)
        reps = 2 * T
        att = _sample_attn(page_table, qx, jnp.tile(k_bt, (1, 1, reps, 1)), jnp.tile(v_bt, (1, 1, reps, 1)),
                           *lam_vecs, sg, ck, cv, layer=l, lam_init=lam_init, pages_per_step=16, n_tok=T)
        att = jnp.swapaxes(att.reshape(Bd, T, ATT_W), 0, 1).reshape(T * Bd, ATT_W).astype(BF16)
        xs = _tail(xs, att, conv, gm, *tail_w, tm=T * Bd)
        k_s.append(k_bt)
        v_s.append(v_bt)
        c_s.append(jnp.swapaxes(cst.reshape(CONV_K - 1, Bd, CONV_W), 0, 1))
        g_s.append(tb(vrows))

    ys = jnp.swapaxes(xs.reshape(T, Bd, D), 0, 1)
    return (xp, ys, jnp.stack(k_p), jnp.stack(v_p), jnp.stack(c_p),
            jnp.stack(k_s), jnp.stack(v_s), jnp.stack(c_s), jnp.stack(g_s))
```

```python
import math
from functools import partial

import jax
import jax.numpy as jnp
from jax import lax
from jax.experimental import pallas as pl
from jax.experimental.pallas import tpu as pltpu

F32 = jnp.float32
BF16 = jnp.bfloat16

EPS = 1e-6
ATT_HEADS = 4
ATT_DH = 64
ATT_DV = 2 * ATT_DH
ATT_W = ATT_HEADS * ATT_DV
QK_W = ATT_HEADS * 2 * ATT_DH
CONV_W = 256
CONV_K = 31
GM_HEADS = 4
GM_DH = 64
GM_W = GM_HEADS * GM_DH
CHUNK = 128
GROUP = 64
GROUP_TILE = 256
HALO = 32
LANES = 128
SUBLANES = 8
FF_CHUNK = 256

_Q0, _K0, _V0, _C0, _M0, _END = 0, QK_W, 2 * QK_W, 2 * QK_W + ATT_W, 2 * QK_W + ATT_W + 2 * CONV_W, \
    2 * QK_W + ATT_W + 2 * CONV_W + 2 * GM_W

Q_SCALE = ATT_DH ** -0.5 * math.log2(math.e)

V7X_VMEM_LIMIT = 56 * 1024 * 1024


def _iota_div(shape, axis, div):
    assert div & (div - 1) == 0
    return lax.broadcasted_iota(jnp.int32, shape, axis) >> (div.bit_length() - 1)


def _iota_mod(shape, axis, mod):
    assert mod & (mod - 1) == 0
    return lax.broadcasted_iota(jnp.int32, shape, axis) & (mod - 1)


def _group_matrix():
    r = _iota_div((GROUP_TILE, GROUP_TILE), 0, GROUP)
    c = _iota_div((GROUP_TILE, GROUP_TILE), 1, GROUP)
    return jnp.where(r == c, 1.0, 0.0).astype(BF16)


def _group_sum(x, gmat):
    return jnp.dot(x.astype(BF16), gmat, preferred_element_type=F32)


def _group_sum_split(x, gmat):
    hi = x.astype(BF16)
    lo = (x - hi.astype(F32)).astype(BF16)
    return (jnp.dot(hi, gmat, preferred_element_type=F32)
            + jnp.dot(lo, gmat, preferred_element_type=F32))


def _rms_rows(x, g):
    return x * lax.rsqrt(jnp.mean(x * x, axis=-1, keepdims=True) + EPS) * g


def _group_rms(z, g, gmat):
    parts = []
    for c in range(0, z.shape[-1], GROUP_TILE):
        zc = z[:, c:c + GROUP_TILE]
        ms = _group_sum(zc * zc, gmat) * (1.0 / GROUP)
        parts.append(zc * lax.rsqrt(ms + EPS))
    y = parts[0] if len(parts) == 1 else jnp.concatenate(parts, axis=-1)
    return y * g


def _group_ln(z, g, b, gmat):
    parts = []
    for c in range(0, z.shape[-1], GROUP_TILE):
        zc = z[:, c:c + GROUP_TILE]
        d = zc - _group_sum_split(zc, gmat) * (1.0 / GROUP)
        var = _group_sum(d * d, gmat) * (1.0 / GROUP)
        parts.append(d * lax.rsqrt(var + EPS))
    y = parts[0] if len(parts) == 1 else jnp.concatenate(parts, axis=-1)
    return y * g + b


def _sigmoid(x):
    return 1.0 / (1.0 + jnp.exp(-x))


def _silu(x):
    return x * _sigmoid(x)


def _gelu_tanh(x):
    return 0.5 * x * (1.0 + jnp.tanh(math.sqrt(2.0 / math.pi) * (x + 0.044715 * (x * x * x))))


def _project(xn, w_ref, lo, hi):
    return jnp.dot(xn, w_ref[:, lo:hi], preferred_element_type=F32)


def _diff_lambda(lq1, lk1, lq2, lk2, lam_init):
    a = jnp.sum(lq1 * lk1, axis=-1, keepdims=True)
    b = jnp.sum(lq2 * lk2, axis=-1, keepdims=True)
    return jnp.exp(a) - jnp.exp(b) + lam_init


def _split_maps(q):
    lane = lax.broadcasted_iota(jnp.int32, q.shape, q.ndim - 1)
    zero = jnp.zeros_like(q)
    return jnp.concatenate([jnp.where(lane < ATT_DH, q, zero), jnp.where(lane >= ATT_DH, q, zero)],
                           axis=q.ndim - 2)


def _nt_dot(a, b):
    return lax.dot_general(a, b, (((1,), (1,)), ((), ())), preferred_element_type=F32)


def _prompt_mix_kernel(x_ref, ng_ref, w_ref, qg_ref, kg_ref, cw_ref, cb_ref, cng_ref, cnb_ref,
                       gng_ref, gnb_ref, ws_ref, bs_ref,
                       q_out, k_out, kb_out, v_out, vb_out, conv_out, gm_out, cst_out,
                       glu_buf, shifted, *, rows_per_conv_chunk):
    j = pl.program_id(1)
    nj = pl.num_programs(1)
    ts = x_ref.shape[1]
    gmat = _group_matrix()

    @pl.when(j == 0)
    def _():
        glu_buf[0:HALO, :] = jnp.zeros((HALO, CONV_W), F32)

    xn = _rms_rows(x_ref[0], ng_ref[...]).astype(BF16)

    c = _project(xn, w_ref, _C0, _M0)
    glu_buf[HALO:HALO + ts, :] = c[:, :CONV_W] * _sigmoid(c[:, CONV_W:])
    first = HALO - (CONV_K - 1)
    n_shift = shifted.shape[1]
    for s in range(1, SUBLANES):
        shifted[s - 1] = glu_buf[s:s + n_shift, :]
    rc = rows_per_conv_chunk

    def conv_rows(r):
        acc = jnp.broadcast_to(cb_ref[...], (rc, CONV_W))
        for t in range(CONV_K):
            base, phase = (first + t) // SUBLANES * SUBLANES, (first + t) % SUBLANES
            src = glu_buf if phase == 0 else shifted.at[phase - 1]
            acc = acc + cw_ref[t:t + 1, :] * src[r + base:r + base + rc, :]
        return acc

    row_starts = list(range(0, ts, rc))
    third = -(-len(row_starts) // 3)
    ys = [conv_rows(r) for r in row_starts[:third]]

    q = _group_rms(_project(xn, w_ref, _Q0, _K0), qg_ref[...], gmat) * Q_SCALE
    q_out[0] = q.astype(BF16)
    ys += [conv_rows(r) for r in row_starts[third:2 * third]]

    k = _group_rms(_project(xn, w_ref, _K0, _V0), kg_ref[...], gmat)
    kb_out[0] = k.astype(BF16)
    ys += [conv_rows(r) for r in row_starts[2 * third:]]

    v = _project(xn, w_ref, _V0, _C0)
    vb_out[0] = v.astype(BF16)
    for h in range(ATT_HEADS):
        k_out[0, pl.ds(h, ts, stride=ATT_HEADS), :] = k[:, h * ATT_DV:(h + 1) * ATT_DV]
        v_out[0, pl.ds(h, ts, stride=ATT_HEADS), :] = v[:, h * ATT_DV:(h + 1) * ATT_DV]

    y = jnp.concatenate(ys, axis=0)
    conv_out[0] = _silu(_group_ln(y, cng_ref[...], cnb_ref[...], gmat)).astype(BF16)
    cst_out[0] = glu_buf[ts + first:ts + HALO, :]
    glu_buf[0:HALO, :] = glu_buf[ts:ts + HALO, :]

    m = _gelu_tanh(_project(xn, w_ref, _M0, _END))
    u = m[:, :GM_W]
    vln = _group_ln(m[:, GM_W:], gng_ref[...], gnb_ref[...], gmat).astype(BF16)
    wr = _iota_mod((GM_HEADS * CHUNK, CHUNK), 0, CHUNK)
    wc = lax.broadcasted_iota(jnp.int32, (GM_HEADS * CHUNK, CHUNK), 1)
    w_tril = jnp.where(wc <= wr, ws_ref[...], 0.0).astype(BF16)
    lane_head = _iota_div((CHUNK, GM_W), 1, GM_DH)
    outs = []
    for r in range(0, ts, CHUNK):
        sv = jnp.dot(w_tril, vln[r:r + CHUNK, :], preferred_element_type=F32)
        s = sv[0:CHUNK, :]
        for h in range(1, GM_HEADS):
            s = jnp.where(lane_head == h, sv[h * CHUNK:(h + 1) * CHUNK, :], s)
        outs.append(u[r:r + CHUNK, :] * (s + bs_ref[...]))
    gm_out[0] = jnp.concatenate(outs, axis=0).astype(BF16)


def _const_spec(shape):
    nd = len(shape)
    return pl.BlockSpec(shape, lambda *_: (0,) * nd, pipeline_mode=pl.Buffered(1))


def _prompt_mix(x, ng, w_in, qg, kg, cw, cb, cng, cnb, gng, gnb, ws, bs, *, ts):
    B, S, D = x.shape
    grid = (B, S // ts)
    row = lambda w: pl.BlockSpec((1, ts, w), lambda b, j: (b, j, 0))
    by_head = pl.BlockSpec((1, ts * ATT_HEADS, ATT_DV), lambda b, j: (b, j, 0))
    outs = (
        jax.ShapeDtypeStruct((B, S, QK_W), BF16),
        jax.ShapeDtypeStruct((B, S * ATT_HEADS, 2 * ATT_DH), F32),
        jax.ShapeDtypeStruct((B, S, QK_W), BF16),
        jax.ShapeDtypeStruct((B, S * ATT_HEADS, ATT_DV), F32),
        jax.ShapeDtypeStruct((B, S, ATT_W), BF16),
        jax.ShapeDtypeStruct((B, S, CONV_W), BF16),
        jax.ShapeDtypeStruct((B, S, GM_W), BF16),
        jax.ShapeDtypeStruct((B, CONV_K - 1, CONV_W), F32),
    )
    return pl.pallas_call(
        partial(_prompt_mix_kernel, rows_per_conv_chunk=64),
        grid=grid,
        in_specs=[row(D)] + [_const_spec(a.shape) for a in (ng, w_in, qg, kg, cw, cb, cng, cnb, gng, gnb, ws, bs)],
        out_specs=(row(QK_W), by_head, row(QK_W), by_head, row(ATT_W), row(CONV_W), row(GM_W),
                   pl.BlockSpec((1, CONV_K - 1, CONV_W), lambda b, j: (b, 0, 0))),
        out_shape=outs,
        scratch_shapes=[pltpu.VMEM((HALO + ts, CONV_W), F32),
                        pltpu.VMEM((SUBLANES - 1, HALO + ts - SUBLANES, CONV_W), F32)],
        compiler_params=pltpu.CompilerParams(
            dimension_semantics=("arbitrary", "arbitrary"), vmem_limit_bytes=V7X_VMEM_LIMIT),
        name="prompt_mix",
    )(x, ng, w_in, qg, kg, cw, cb, cng, cnb, gng, gnb, ws, bs)


def _sample_mix_kernel(x_ref, st_ref, ng_ref, w_ref, qg_ref, kg_ref, cw_ref, cb_ref, cng_ref, cnb_ref,
                       gng_ref, gnb_ref, wl_ref, bl_ref,
                       q_out, k_out, v_out, conv_out, gm_out, cst_out, vrow_out, buf, *, n_tok, n_batch):
    rows = n_tok * n_batch
    hist = (CONV_K - 1) * n_batch
    gmat = _group_matrix()
    xn = _rms_rows(x_ref[...], ng_ref[...]).astype(BF16)

    q_out[...] = _group_rms(_project(xn, w_ref, _Q0, _K0), qg_ref[...], gmat) * Q_SCALE
    k_out[...] = _group_rms(_project(xn, w_ref, _K0, _V0), kg_ref[...], gmat)
    v_out[...] = _project(xn, w_ref, _V0, _C0)

    c = _project(xn, w_ref, _C0, _M0)
    buf[0:hist, :] = st_ref[...]
    buf[hist:hist + rows, :] = c[:, :CONV_W] * _sigmoid(c[:, CONV_W:])
    acc = jnp.broadcast_to(cb_ref[...], (rows, CONV_W))
    for t in range(CONV_K):
        acc = acc + cw_ref[t:t + 1, :] * buf[t * n_batch:t * n_batch + rows, :]
    conv_out[...] = _silu(_group_ln(acc, cng_ref[...], cnb_ref[...], gmat)).astype(BF16)
    cst_out[...] = buf[rows:rows + hist, :]

    m = _gelu_tanh(_project(xn, w_ref, _M0, _END))
    u = m[:, :GM_W]
    vln = _group_ln(m[:, GM_W:], gng_ref[...], gnb_ref[...], gmat)
    vrow_out[...] = vln
    outs = []
    for i in range(n_tok):
        s = jnp.broadcast_to(bl_ref[i:i + 1, :], (n_batch, GM_W))
        for jj in range(i + 1):
            s = s + wl_ref[i * n_tok + jj:i * n_tok + jj + 1, :] * vln[jj * n_batch:(jj + 1) * n_batch, :]
        outs.append(u[i * n_batch:(i + 1) * n_batch, :] * s)
    gm_out[...] = jnp.concatenate(outs, axis=0).astype(BF16)


def _sample_mix(x, st, ng, w_in, qg, kg, cw, cb, cng, cnb, gng, gnb, wl, bl, *, n_tok, n_batch):
    rows = n_tok * n_batch
    hist = (CONV_K - 1) * n_batch
    outs = (
        jax.ShapeDtypeStruct((rows, QK_W), F32),
        jax.ShapeDtypeStruct((rows, QK_W), F32),
        jax.ShapeDtypeStruct((rows, ATT_W), F32),
        jax.ShapeDtypeStruct((rows, CONV_W), BF16),
        jax.ShapeDtypeStruct((rows, GM_W), BF16),
        jax.ShapeDtypeStruct((hist, CONV_W), F32),
        jax.ShapeDtypeStruct((rows, GM_W), F32),
    )
    return pl.pallas_call(
        partial(_sample_mix_kernel, n_tok=n_tok, n_batch=n_batch),
        out_shape=outs,
        scratch_shapes=[pltpu.VMEM((hist + rows, CONV_W), F32)],
        compiler_params=pltpu.CompilerParams(vmem_limit_bytes=V7X_VMEM_LIMIT),
        name="sample_mix",
    )(x, st, ng, w_in, qg, kg, cw, cb, cng, cnb, gng, gnb, wl, bl)


def _lane_tiles(x):
    return [x[:, t:t + LANES] for t in range(0, x.shape[-1], LANES)]


def _prompt_attn_kernel(q_ref, k_ref, v_ref, lq1_ref, lk1_ref, lq2_ref, lk2_ref, sg_ref, o_ref, o_scr, *,
                        lam_init, chunk):
    i = pl.program_id(2)
    tq = q_ref.shape[1]
    q2 = _split_maps(q_ref[0])
    lam = _diff_lambda(lq1_ref[...], lk1_ref[...], lq2_ref[...], lk2_ref[...], lam_init)

    def scores(r0, n_keys):
        s = _nt_dot(q2[r0:r0 + chunk], k_ref[0, 0:n_keys, :])
        tiles = _lane_tiles(s)
        n_diag = chunk // LANES
        for t in range(n_diag):
            rpos = lax.broadcasted_iota(jnp.int32, (chunk, LANES), 0)
            cpos = lax.broadcasted_iota(jnp.int32, (chunk, LANES), 1) + t * LANES
            tiles[t - n_diag] = jnp.where(cpos <= rpos, tiles[t - n_diag], -jnp.inf)
        return tiles

    def softmax_pv(r0, n_keys, tiles):
        m_tile = tiles[0]
        for t in tiles[1:]:
            m_tile = jnp.maximum(m_tile, t)
        m = jnp.broadcast_to(jnp.max(m_tile, axis=-1, keepdims=True), (chunk, LANES))
        ps = [jnp.exp2(t - m) for t in tiles]
        l_tile = ps[0]
        for p in ps[1:]:
            l_tile = l_tile + p
        acc = jnp.dot(jnp.concatenate(ps, axis=-1).astype(BF16), v_ref[0, 0:n_keys, :],
                      preferred_element_type=F32)
        o_scr[r0:r0 + chunk, :] = acc / jnp.sum(l_tile, axis=-1, keepdims=True)

    def attend(blk):
        work = [(r0, ((blk * tq + r0 % tq) // chunk + 1) * chunk) for r0 in range(0, 2 * tq, chunk)]
        ahead = 2
        pending = [scores(*w) for w in work[:ahead]]
        for idx, (r0, n_keys) in enumerate(work):
            if idx + ahead < len(work):
                pending.append(scores(*work[idx + ahead]))
            softmax_pv(r0, n_keys, pending.pop(0))
        w = o_scr[0:tq, :] - lam * o_scr[tq:2 * tq, :]
        o_ref[0] = (_rms_rows(w, sg_ref[...]) * (1.0 - lam_init)).astype(BF16)

    for blk in range(k_ref.shape[1] // tq):
        pl.when(i == blk)(partial(attend, blk))


def _prompt_attn(q, kb, vb, lq1, lk1, lq2, lk2, sg, *, lam_init, tq):
    B, S, _ = q.shape
    grid = (B, ATT_HEADS, S // tq)
    vec = lambda a: pl.BlockSpec(a.shape, lambda b, h, i: (0, 0))
    return pl.pallas_call(
        partial(_prompt_attn_kernel, lam_init=lam_init, chunk=256),
        grid=grid,
        in_specs=[pl.BlockSpec((1, tq, ATT_DV), lambda b, h, i: (b, i, h)),
                  pl.BlockSpec((1, S, ATT_DV), lambda b, h, i: (b, 0, h)),
                  pl.BlockSpec((1, S, ATT_DV), lambda b, h, i: (b, 0, h)),
                  vec(lq1), vec(lk1), vec(lq2), vec(lk2), vec(sg)],
        out_specs=pl.BlockSpec((1, tq, ATT_DV), lambda b, h, i: (b, i, h)),
        out_shape=jax.ShapeDtypeStruct((B, S, ATT_W), BF16),
        scratch_shapes=[pltpu.VMEM((2 * tq, ATT_DV), F32)],
        compiler_params=pltpu.CompilerParams(
            dimension_semantics=("arbitrary", "arbitrary", "arbitrary"), vmem_limit_bytes=V7X_VMEM_LIMIT),
        name="prompt_attn",
    )(q, kb, vb, lq1, lk1, lq2, lk2, sg)


def _sample_attn_kernel(pt_ref, qx_ref, kn_ref, vn_ref, lq1_ref, lk1_ref, lq2_ref, lk2_ref, sg_ref, *rest,
                        lam_init, pages_per_step, n_tok):
    g = pages_per_step
    k_refs, v_refs = rest[:g], rest[g:2 * g]
    o_ref, m_scr, l_scr, acc_scr = rest[2 * g:]
    p_idx = pl.program_id(1)
    rows = qx_ref.shape[1]
    half = rows // 2
    page_rows = k_refs[0].shape[0]
    qx = qx_ref[0]

    @pl.when(p_idx == 0)
    def _():
        qf = qx.astype(F32)
        tok = _iota_mod((rows, 1), 0, half) >> (ATT_HEADS.bit_length() - 1)
        ss = []
        for j in range(n_tok):
            sj = jnp.sum(qf * kn_ref[0, j], axis=-1, keepdims=True)
            ss.append(jnp.where(tok >= j, sj, -jnp.inf))
        m0 = ss[0]
        for sj in ss[1:]:
            m0 = jnp.maximum(m0, sj)
        l0 = jnp.zeros((rows, 1), F32)
        a0 = jnp.zeros((rows, ATT_DV), F32)
        for j in range(n_tok):
            pj = jnp.exp2(ss[j] - m0)
            l0 = l0 + pj
            a0 = a0 + pj * vn_ref[0, j]
        m_scr[...] = jnp.broadcast_to(m0, (rows, LANES))
        l_scr[...] = jnp.where(lax.broadcasted_iota(jnp.int32, (rows, LANES), 1) == 0, l0, 0.0)
        acc_scr[...] = a0

    own = _iota_mod((rows, LANES), 1, ATT_HEADS) == _iota_mod((rows, LANES), 0, ATT_HEADS)
    page_tiles = []
    for i in range(g):
        s = _nt_dot(qx, k_refs[i][...].astype(BF16))
        page_tiles.append([jnp.where(own, t, -jnp.inf) for t in _lane_tiles(s)])
    m_tile = None
    for tiles in page_tiles:
        for t in tiles:
            m_tile = t if m_tile is None else jnp.maximum(m_tile, t)
    m_prev = m_scr[...]
    m_new = jnp.maximum(m_prev, jnp.broadcast_to(jnp.max(m_tile, axis=-1, keepdims=True), (rows, LANES)))
    alpha = jnp.exp2(m_prev - m_new)
    l_tile = alpha * l_scr[...]
    acc = alpha * acc_scr[...]
    for i in range(g):
        ps = [jnp.exp2(t - m_new) for t in page_tiles[i]]
        for p in ps:
            l_tile = l_tile + p
        acc = acc + jnp.dot(jnp.concatenate(ps, axis=-1).astype(BF16), v_refs[i][...].astype(BF16),
                            preferred_element_type=F32)
    m_scr[...] = m_new
    l_scr[...] = l_tile
    acc_scr[...] = acc

    @pl.when(p_idx == pl.num_programs(1) - 1)
    def _():
        o = acc_scr[...] / jnp.sum(l_scr[...], axis=-1, keepdims=True)
        lam = _diff_lambda(lq1_ref[...], lk1_ref[...], lq2_ref[...], lk2_ref[...], lam_init)
        w = o[:half] - lam * o[half:]
        o_ref[0] = _rms_rows(w, sg_ref[...]) * (1.0 - lam_init)


def _sample_attn(page_table, qx, kn, vn, lq1, lk1, lq2, lk2, sg, cache_k, cache_v, *,
                 layer, lam_init, pages_per_step, n_tok):
    Bd, rows, _ = qx.shape
    n_pages = page_table.shape[1]
    page_rows = cache_k.shape[2]
    g = pages_per_step
    grid = (Bd, n_pages // g)
    pt_flat = page_table.reshape(-1)
    vec = lambda a: pl.BlockSpec(a.shape, lambda b, p, pt: (0, 0))
    per_b = lambda a: pl.BlockSpec((1,) + a.shape[1:], lambda b, p, pt: (b,) + (0,) * (a.ndim - 1))

    def page_spec(i):
        return pl.BlockSpec((None, None, page_rows, ATT_DV),
                            lambda b, p, pt: (layer, pt[b * n_pages + p * g + i], 0, 0))

    in_specs = ([per_b(qx), per_b(kn), per_b(vn), vec(lq1), vec(lk1), vec(lq2), vec(lk2), vec(sg)]
                + [page_spec(i) for i in range(g)] * 2)
    return pl.pallas_call(
        partial(_sample_attn_kernel, lam_init=lam_init, pages_per_step=g, n_tok=n_tok),
        grid_spec=pltpu.PrefetchScalarGridSpec(
            num_scalar_prefetch=1, grid=grid, in_specs=in_specs,
            out_specs=pl.BlockSpec((1, rows // 2, ATT_DV), lambda b, p, pt: (b, 0, 0)),
            scratch_shapes=[pltpu.VMEM((rows, LANES), F32), pltpu.VMEM((rows, LANES), F32),
                            pltpu.VMEM((rows, ATT_DV), F32)]),
        out_shape=jax.ShapeDtypeStruct((Bd, rows // 2, ATT_DV), F32),
        compiler_params=pltpu.CompilerParams(
            dimension_semantics=("arbitrary", "arbitrary"), vmem_limit_bytes=V7X_VMEM_LIMIT),
        name="sample_attn",
    )(pt_flat, qx, kn, vn, lq1, lk1, lq2, lk2, sg, *([cache_k] * g), *([cache_v] * g))


def _tail_kernel(x_ref, att_ref, conv_ref, gm_ref, wo_ref, fg_ref, wg_ref, wu_ref, wd_ref, y_ref, act_scr):
    mix = (jnp.dot(att_ref[...], wo_ref[0:ATT_W, :], preferred_element_type=F32)
           + jnp.dot(conv_ref[...], wo_ref[ATT_W:ATT_W + CONV_W, :], preferred_element_type=F32)
           + jnp.dot(gm_ref[...], wo_ref[ATT_W + CONV_W:, :], preferred_element_type=F32))
    x1 = x_ref[...] + mix
    h = _rms_rows(x1, fg_ref[...]).astype(BF16)
    for c in range(0, wg_ref.shape[1], FF_CHUNK):
        gate = jnp.dot(h, wg_ref[:, c:c + FF_CHUNK], preferred_element_type=F32)
        up = jnp.dot(h, wu_ref[:, c:c + FF_CHUNK], preferred_element_type=F32)
        act_scr[:, c:c + FF_CHUNK] = (_silu(gate) * up).astype(BF16)
    y_ref[...] = x1 + jnp.dot(act_scr[...], wd_ref[...], preferred_element_type=F32)


def _tail(x, att, conv, gm, wo, fg, wg, wu, wd, *, tm):
    n, d = x.shape
    d_ff = wg.shape[1]
    assert d_ff % FF_CHUNK == 0
    row = lambda w: pl.BlockSpec((tm, w), lambda i: (i, 0))
    return pl.pallas_call(
        _tail_kernel,
        grid=(n // tm,),
        in_specs=[row(d), row(ATT_W), row(CONV_W), row(GM_W)]
                 + [_const_spec(a.shape) for a in (wo, fg, wg, wu, wd)],
        out_specs=row(d),
        out_shape=jax.ShapeDtypeStruct((n, d), F32),
        scratch_shapes=[pltpu.VMEM((tm, d_ff), BF16)],
        compiler_params=pltpu.CompilerParams(
            dimension_semantics=("arbitrary",), vmem_limit_bytes=V7X_VMEM_LIMIT),
        name="tail",
    )(x, att, conv, gm, wo, fg, wg, wu, wd)


def kernel(x_prompt, x_sample, cache_k, cache_v, state_conv, page_table, mix_norm_g, w_in, q_norm_g, k_norm_g,
           lam_q1, lam_k1, lam_q2, lam_k2, subln_g, conv_w, conv_b, conv_norm_g, conv_norm_b, gm_norm_g,
           gm_norm_b, gm_ws, gm_bs, w_out, ffn_norm_g, w_gate, w_up, w_down):
    B, S, D = x_prompt.shape
    Bd, T, _ = x_sample.shape
    depth = w_in.shape[0]
    n_phys, page = cache_k.shape[1], cache_k.shape[2]
    assert cache_k.shape[3:] == (ATT_HEADS, 2 * ATT_DH) and cache_v.shape[3:] == (ATT_HEADS, ATT_DV)
    assert w_in.shape[-1] == _END and conv_w.shape[1:] == (CONV_K, CONV_W)
    assert gm_ws.shape[1:] == (GM_HEADS, CHUNK, CHUNK) and S % CHUNK == 0 and T <= CHUNK

    ck = cache_k.reshape(depth, n_phys, page * ATT_HEADS, 2 * ATT_DH)
    cv = cache_v.reshape(depth, n_phys, page * ATT_HEADS, ATT_DV)
    row = lambda a: a.reshape(1, -1)

    xp = x_prompt
    xs = jnp.swapaxes(x_sample, 0, 1).reshape(T * Bd, D)

    k_p, v_p, c_p, k_s, v_s, c_s, g_s = [], [], [], [], [], [], []
    for l in range(depth):
        lam_init = 0.8 - 0.6 * math.exp(-0.3 * l)
        w_in_b = w_in[l].astype(BF16)
        qg = row(jnp.tile(q_norm_g[l], QK_W // ATT_DH))
        kg = row(jnp.tile(k_norm_g[l], QK_W // ATT_DH))
        lam_vecs = (row(lam_q1[l]), row(lam_k1[l]), row(lam_q2[l]), row(lam_k2[l]))
        sg = row(subln_g[l])
        mix_args = (row(mix_norm_g[l]), w_in_b, qg, kg, conv_w[l], row(conv_b[l]), row(conv_norm_g[l]),
                    row(conv_norm_b[l]), row(gm_norm_g[l]), row(gm_norm_b[l]))
        tail_w = (w_out[l].astype(BF16), row(ffn_norm_g[l]), w_gate[l].astype(BF16), w_up[l].astype(BF16),
                  w_down[l].astype(BF16))

        ws_stack = gm_ws[l].reshape(GM_HEADS * CHUNK, CHUNK)
        bs_rows = jnp.repeat(gm_bs[l].T, GM_DH, axis=1)
        q, k, kb, v, vb, conv, gm, cst = _prompt_mix(xp, *mix_args, ws_stack, bs_rows, ts=512)
        att = _prompt_attn(q, kb, vb, *lam_vecs, sg, lam_init=lam_init, tq=512)
        xp = _tail(xp.reshape(B * S, D), att.reshape(B * S, ATT_W), conv.reshape(B * S, CONV_W),
                   gm.reshape(B * S, GM_W), *tail_w, tm=512).reshape(B, S, D)
        k_p.append(k.reshape(B, S, ATT_HEADS, 2 * ATT_DH))
        v_p.append(v.reshape(B, S, ATT_HEADS, ATT_DV))
        c_p.append(cst)

        st = jnp.swapaxes(state_conv[l], 0, 1).reshape((CONV_K - 1) * Bd, CONV_W)
        wl = jnp.repeat(jnp.transpose(gm_ws[l][:, :T, :T], (1, 2, 0)).reshape(T * T, GM_HEADS), GM_DH, axis=1)
        bl = jnp.repeat(gm_bs[l][:, :T].T, GM_DH, axis=1)
        qs, ks, vs, conv, gm, cst, vrows = _sample_mix(xs, st, *mix_args, wl, bl, n_tok=T, n_batch=Bd)
        tb = lambda a: jnp.swapaxes(a.reshape(T, Bd, -1), 0, 1)
        k_bt = tb(ks).reshape(Bd, T, ATT_HEADS, 2 * ATT_DH)
        v_bt = tb(vs).reshape(Bd, T, ATT_HEADS, ATT_DV)
        qx = _split_maps(tb(qs).reshape(Bd, T * ATT_HEADS, 2 * ATT_DH)).astype(BF16)
        reps = 2 * T
        att = _sample_attn(page_table, qx, jnp.tile(k_bt, (1, 1, reps, 1)), jnp.tile(v_bt, (1, 1, reps, 1)),
                           *lam_vecs, sg, ck, cv, layer=l, lam_init=lam_init, pages_per_step=16, n_tok=T)
        att = jnp.swapaxes(att.reshape(Bd, T, ATT_W), 0, 1).reshape(T * Bd, ATT_W).astype(BF16)
        xs = _tail(xs, att, conv, gm, *tail_w, tm=T * Bd)
        k_s.append(k_bt)
        v_s.append(v_bt)
        c_s.append(jnp.swapaxes(cst.reshape(CONV_K - 1, Bd, CONV_W), 0, 1))
        g_s.append(tb(vrows))

    ys = jnp.swapaxes(xs.reshape(T, Bd, D), 0, 1)
    return (xp, ys, jnp.stack(k_p), jnp.stack(v_p), jnp.stack(c_p),
            jnp.stack(k_s), jnp.stack(v_s), jnp.stack(c_s), jnp.stack(g_s))
```

```python
import math
from functools import partial

import jax
import jax.numpy as jnp
from jax import lax
from jax.experimental import pallas as pl
from jax.experimental.pallas import tpu as pltpu

F32 = jnp.float32
BF16 = jnp.bfloat16

EPS = 1e-6
ATT_HEADS = 4
ATT_DH = 64
ATT_DV = 2 * ATT_DH
ATT_W = ATT_HEADS * ATT_DV
QK_W = ATT_HEADS * 2 * ATT_DH
CONV_W = 256
CONV_K = 31
GM_HEADS = 4
GM_DH = 64
GM_W = GM_HEADS * GM_DH
CHUNK = 128
GROUP = 64
GROUP_TILE = 256
HALO = 32
LANES = 128
SUBLANES = 8
FF_CHUNK = 256

_Q0, _K0, _V0, _C0, _M0, _END = 0, QK_W, 2 * QK_W, 2 * QK_W + ATT_W, 2 * QK_W + ATT_W + 2 * CONV_W, \
    2 * QK_W + ATT_W + 2 * CONV_W + 2 * GM_W

Q_SCALE = ATT_DH ** -0.5 * math.log2(math.e)

V7X_VMEM_LIMIT = 56 * 1024 * 1024


def _iota_div(shape, axis, div):
    assert div & (div - 1) == 0
    return lax.broadcasted_iota(jnp.int32, shape, axis) >> (div.bit_length() - 1)


def _iota_mod(shape, axis, mod):
    assert mod & (mod - 1) == 0
    return lax.broadcasted_iota(jnp.int32, shape, axis) & (mod - 1)


def _group_matrix():
    r = _iota_div((GROUP_TILE, GROUP_TILE), 0, GROUP)
    c = _iota_div((GROUP_TILE, GROUP_TILE), 1, GROUP)
    return jnp.where(r == c, 1.0, 0.0).astype(BF16)


def _group_sum(x, gmat):
    return jnp.dot(x.astype(BF16), gmat, preferred_element_type=F32)


def _group_sum_split(x, gmat):
    hi = x.astype(BF16)
    lo = (x - hi.astype(F32)).astype(BF16)
    return (jnp.dot(hi, gmat, preferred_element_type=F32)
            + jnp.dot(lo, gmat, preferred_element_type=F32))


def _rms_rows(x, g):
    return x * lax.rsqrt(jnp.mean(x * x, axis=-1, keepdims=True) + EPS) * g


def _group_rms(z, g, gmat):
    parts = []
    for c in range(0, z.shape[-1], GROUP_TILE):
        zc = z[:, c:c + GROUP_TILE]
        ms = _group_sum(zc * zc, gmat) * (1.0 / GROUP)
        parts.append(zc * lax.rsqrt(ms + EPS))
    y = parts[0] if len(parts) == 1 else jnp.concatenate(parts, axis=-1)
    return y * g


def _group_ln(z, g, b, gmat):
    parts = []
    for c in range(0, z.shape[-1], GROUP_TILE):
        zc = z[:, c:c + GROUP_TILE]
        d = zc - _group_sum_split(zc, gmat) * (1.0 / GROUP)
        var = _group_sum(d * d, gmat) * (1.0 / GROUP)
        parts.append(d * lax.rsqrt(var + EPS))
    y = parts[0] if len(parts) == 1 else jnp.concatenate(parts, axis=-1)
    return y * g + b


def _sigmoid(x):
    return 1.0 / (1.0 + jnp.exp(-x))


def _silu(x):
    return x * _sigmoid(x)


def _gelu_tanh(x):
    return 0.5 * x * (1.0 + jnp.tanh(math.sqrt(2.0 / math.pi) * (x + 0.044715 * (x * x * x))))


def _project(xn, w_ref, lo, hi):
    return jnp.dot(xn, w_ref[:, lo:hi], preferred_element_type=F32)


def _diff_lambda(lq1, lk1, lq2, lk2, lam_init):
    a = jnp.sum(lq1 * lk1, axis=-1, keepdims=True)
    b = jnp.sum(lq2 * lk2, axis=-1, keepdims=True)
    return jnp.exp(a) - jnp.exp(b) + lam_init


def _split_maps(q):
    lane = lax.broadcasted_iota(jnp.int32, q.shape, q.ndim - 1)
    zero = jnp.zeros_like(q)
    return jnp.concatenate([jnp.where(lane < ATT_DH, q, zero), jnp.where(lane >= ATT_DH, q, zero)],
                           axis=q.ndim - 2)


def _nt_dot(a, b):
    return lax.dot_general(a, b, (((1,), (1,)), ((), ())), preferred_element_type=F32)


def _prompt_mix_kernel(x_ref, ng_ref, w_ref, qg_ref, kg_ref, cw_ref, cb_ref, cng_ref, cnb_ref,
                       gng_ref, gnb_ref, ws_ref, bs_ref,
                       q_out, k_out, kb_out, v_out, vb_out, conv_out, gm_out, cst_out,
                       glu_buf, shifted, *, rows_per_conv_chunk):
    j = pl.program_id(1)
    nj = pl.num_programs(1)
    ts = x_ref.shape[1]
    gmat = _group_matrix()

    @pl.when(j == 0)
    def _():
        glu_buf[0:HALO, :] = jnp.zeros((HALO, CONV_W), F32)

    xn = _rms_rows(x_ref[0], ng_ref[...]).astype(BF16)

    c = _project(xn, w_ref, _C0, _M0)
    glu_buf[HALO:HALO + ts, :] = c[:, :CONV_W] * _sigmoid(c[:, CONV_W:])
    first = HALO - (CONV_K - 1)
    n_shift = shifted.shape[1]
    for s in range(1, SUBLANES):
        shifted[s - 1] = glu_buf[s:s + n_shift, :]
    rc = rows_per_conv_chunk

    def conv_rows(r):
        acc = jnp.broadcast_to(cb_ref[...], (rc, CONV_W))
        for t in range(CONV_K):
            base, phase = (first + t) // SUBLANES * SUBLANES, (first + t) % SUBLANES
            src = glu_buf if phase == 0 else shifted.at[phase - 1]
            acc = acc + cw_ref[t:t + 1, :] * src[r + base:r + base + rc, :]
        return acc

    row_starts = list(range(0, ts, rc))
    third = -(-len(row_starts) // 3)
    ys = [conv_rows(r) for r in row_starts[:third]]

    q = _group_rms(_project(xn, w_ref, _Q0, _K0), qg_ref[...], gmat) * Q_SCALE
    q_out[0] = q.astype(BF16)
    ys += [conv_rows(r) for r in row_starts[third:2 * third]]

    k = _group_rms(_project(xn, w_ref, _K0, _V0), kg_ref[...], gmat)
    kb_out[0] = k.astype(BF16)
    ys += [conv_rows(r) for r in row_starts[2 * third:]]

    v = _project(xn, w_ref, _V0, _C0)
    vb_out[0] = v.astype(BF16)
    for slab in range(k_out.shape[0]):
        for h in range(ATT_HEADS):
            k_out[slab, pl.ds(h, ts, stride=ATT_HEADS), :] = k[:, h * ATT_DV:(h + 1) * ATT_DV]
            v_out[slab, pl.ds(h, ts, stride=ATT_HEADS), :] = v[:, h * ATT_DV:(h + 1) * ATT_DV]

    y = jnp.concatenate(ys, axis=0)
    conv_out[0] = _silu(_group_ln(y, cng_ref[...], cnb_ref[...], gmat)).astype(BF16)
    cst_out[0] = glu_buf[ts + first:ts + HALO, :]
    glu_buf[0:HALO, :] = glu_buf[ts:ts + HALO, :]

    m = _gelu_tanh(_project(xn, w_ref, _M0, _END))
    u = m[:, :GM_W]
    vln = _group_ln(m[:, GM_W:], gng_ref[...], gnb_ref[...], gmat).astype(BF16)
    wr = _iota_mod((GM_HEADS * CHUNK, CHUNK), 0, CHUNK)
    wc = lax.broadcasted_iota(jnp.int32, (GM_HEADS * CHUNK, CHUNK), 1)
    w_tril = jnp.where(wc <= wr, ws_ref[...], 0.0).astype(BF16)
    lane_head = _iota_div((CHUNK, GM_W), 1, GM_DH)
    outs = []
    for r in range(0, ts, CHUNK):
        sv = jnp.dot(w_tril, vln[r:r + CHUNK, :], preferred_element_type=F32)
        s = sv[0:CHUNK, :]
        for h in range(1, GM_HEADS):
            s = jnp.where(lane_head == h, sv[h * CHUNK:(h + 1) * CHUNK, :], s)
        outs.append(u[r:r + CHUNK, :] * (s + bs_ref[...]))
    gm_out[0] = jnp.concatenate(outs, axis=0).astype(BF16)


def _const_spec(shape):
    nd = len(shape)
    return pl.BlockSpec(shape, lambda *_: (0,) * nd, pipeline_mode=pl.Buffered(1))


def _prompt_mix_kernel_with_kv(*refs, n_in, rows_per_conv_chunk):
    _prompt_mix_kernel(*refs[:n_in], *refs[n_in + 2:], rows_per_conv_chunk=rows_per_conv_chunk)


def _prompt_mix(x, ng, w_in, qg, kg, cw, cb, cng, cnb, gng, gnb, ws, bs, *, ts, layer, depth, kv_all=None):
    B, S, D = x.shape
    grid = (B, S // ts)
    row = lambda w: pl.BlockSpec((1, ts, w), lambda b, j: (b, j, 0))
    if kv_all is None:
        by_head = pl.BlockSpec((depth, None, ts * ATT_HEADS, ATT_DV), lambda b, j: (0, b, j, 0))
    else:
        by_head = pl.BlockSpec((1, None, ts * ATT_HEADS, ATT_DV), lambda b, j: (layer, b, j, 0))
    inputs = (x, ng, w_in, qg, kg, cw, cb, cng, cnb, gng, gnb, ws, bs)
    in_specs = [row(D)] + [_const_spec(a.shape) for a in inputs[1:]]
    body = partial(_prompt_mix_kernel, rows_per_conv_chunk=64)
    aliases = {}
    if kv_all is not None:
        body = partial(_prompt_mix_kernel_with_kv, n_in=len(inputs), rows_per_conv_chunk=64)
        aliases = {len(inputs): 1, len(inputs) + 1: 3}
        in_specs = in_specs + [pl.BlockSpec(memory_space=pl.ANY)] * 2
        inputs = inputs + tuple(kv_all)
    outs = (
        jax.ShapeDtypeStruct((B, S, QK_W), BF16),
        jax.ShapeDtypeStruct((depth, B, S * ATT_HEADS, 2 * ATT_DH), F32),
        jax.ShapeDtypeStruct((B, S, QK_W), BF16),
        jax.ShapeDtypeStruct((depth, B, S * ATT_HEADS, ATT_DV), F32),
        jax.ShapeDtypeStruct((B, S, ATT_W), BF16),
        jax.ShapeDtypeStruct((B, S, CONV_W), BF16),
        jax.ShapeDtypeStruct((B, S, GM_W), BF16),
        jax.ShapeDtypeStruct((B, CONV_K - 1, CONV_W), F32),
    )
    return pl.pallas_call(
        body,
        grid=grid,
        in_specs=in_specs,
        out_specs=(row(QK_W), by_head, row(QK_W), by_head, row(ATT_W), row(CONV_W), row(GM_W),
                   pl.BlockSpec((1, CONV_K - 1, CONV_W), lambda b, j: (b, 0, 0))),
        out_shape=outs,
        input_output_aliases=aliases,
        scratch_shapes=[pltpu.VMEM((HALO + ts, CONV_W), F32),
                        pltpu.VMEM((SUBLANES - 1, HALO + ts - SUBLANES, CONV_W), F32)],
        compiler_params=pltpu.CompilerParams(
            dimension_semantics=("arbitrary", "arbitrary"), vmem_limit_bytes=V7X_VMEM_LIMIT),
        name="prompt_mix",
    )(*inputs)


def _sample_mix_kernel(x_ref, st_ref, ng_ref, w_ref, qg_ref, kg_ref, cw_ref, cb_ref, cng_ref, cnb_ref,
                       gng_ref, gnb_ref, wl_ref, bl_ref,
                       q_out, k_out, v_out, conv_out, gm_out, cst_out, vrow_out, buf, *, n_tok, n_batch):
    rows = n_tok * n_batch
    hist = (CONV_K - 1) * n_batch
    gmat = _group_matrix()
    xn = _rms_rows(x_ref[...], ng_ref[...]).astype(BF16)

    q_out[...] = _group_rms(_project(xn, w_ref, _Q0, _K0), qg_ref[...], gmat) * Q_SCALE
    k_out[...] = _group_rms(_project(xn, w_ref, _K0, _V0), kg_ref[...], gmat)
    v_out[...] = _project(xn, w_ref, _V0, _C0)

    c = _project(xn, w_ref, _C0, _M0)
    buf[0:hist, :] = st_ref[...]
    buf[hist:hist + rows, :] = c[:, :CONV_W] * _sigmoid(c[:, CONV_W:])
    acc = jnp.broadcast_to(cb_ref[...], (rows, CONV_W))
    for t in range(CONV_K):
        acc = acc + cw_ref[t:t + 1, :] * buf[t * n_batch:t * n_batch + rows, :]
    conv_out[...] = _silu(_group_ln(acc, cng_ref[...], cnb_ref[...], gmat)).astype(BF16)
    cst_out[...] = buf[rows:rows + hist, :]

    m = _gelu_tanh(_project(xn, w_ref, _M0, _END))
    u = m[:, :GM_W]
    vln = _group_ln(m[:, GM_W:], gng_ref[...], gnb_ref[...], gmat)
    vrow_out[...] = vln
    outs = []
    for i in range(n_tok):
        s = jnp.broadcast_to(bl_ref[i:i + 1, :], (n_batch, GM_W))
        for jj in range(i + 1):
            s = s + wl_ref[i * n_tok + jj:i * n_tok + jj + 1, :] * vln[jj * n_batch:(jj + 1) * n_batch, :]
        outs.append(u[i * n_batch:(i + 1) * n_batch, :] * s)
    gm_out[...] = jnp.concatenate(outs, axis=0).astype(BF16)


def _sample_mix(x, st, ng, w_in, qg, kg, cw, cb, cng, cnb, gng, gnb, wl, bl, *, n_tok, n_batch):
    rows = n_tok * n_batch
    hist = (CONV_K - 1) * n_batch
    outs = (
        jax.ShapeDtypeStruct((rows, QK_W), F32),
        jax.ShapeDtypeStruct((rows, QK_W), F32),
        jax.ShapeDtypeStruct((rows, ATT_W), F32),
        jax.ShapeDtypeStruct((rows, CONV_W), BF16),
        jax.ShapeDtypeStruct((rows, GM_W), BF16),
        jax.ShapeDtypeStruct((hist, CONV_W), F32),
        jax.ShapeDtypeStruct((rows, GM_W), F32),
    )
    return pl.pallas_call(
        partial(_sample_mix_kernel, n_tok=n_tok, n_batch=n_batch),
        out_shape=outs,
        scratch_shapes=[pltpu.VMEM((hist + rows, CONV_W), F32)],
        compiler_params=pltpu.CompilerParams(vmem_limit_bytes=V7X_VMEM_LIMIT),
        name="sample_mix",
    )(x, st, ng, w_in, qg, kg, cw, cb, cng, cnb, gng, gnb, wl, bl)


def _lane_tiles(x):
    return [x[:, t:t + LANES] for t in range(0, x.shape[-1], LANES)]


def _prompt_attn_kernel(q_ref, k_ref, v_ref, lq1_ref, lk1_ref, lq2_ref, lk2_ref, sg_ref, o_ref, o_scr, *,
                        lam_init, chunk):
    i = pl.program_id(2)
    tq = q_ref.shape[1]
    q2 = _split_maps(q_ref[0])
    lam = _diff_lambda(lq1_ref[...], lk1_ref[...], lq2_ref[...], lk2_ref[...], lam_init)

    def scores(r0, n_keys):
        s = _nt_dot(q2[r0:r0 + chunk], k_ref[0, 0:n_keys, :])
        tiles = _lane_tiles(s)
        n_diag = chunk // LANES
        for t in range(n_diag):
            rpos = lax.broadcasted_iota(jnp.int32, (chunk, LANES), 0)
            cpos = lax.broadcasted_iota(jnp.int32, (chunk, LANES), 1) + t * LANES
            tiles[t - n_diag] = jnp.where(cpos <= rpos, tiles[t - n_diag], -jnp.inf)
        return tiles

    def softmax_pv(r0, n_keys, tiles):
        m_tile = tiles[0]
        for t in tiles[1:]:
            m_tile = jnp.maximum(m_tile, t)
        m = jnp.broadcast_to(jnp.max(m_tile, axis=-1, keepdims=True), (chunk, LANES))
        ps = [jnp.exp2(t - m) for t in tiles]
        l_tile = ps[0]
        for p in ps[1:]:
            l_tile = l_tile + p
        acc = jnp.dot(jnp.concatenate(ps, axis=-1).astype(BF16), v_ref[0, 0:n_keys, :],
                      preferred_element_type=F32)
        o_scr[r0:r0 + chunk, :] = acc / jnp.sum(l_tile, axis=-1, keepdims=True)

    def attend(blk):
        work = [(r0, ((blk * tq + r0 % tq) // chunk + 1) * chunk) for r0 in range(0, 2 * tq, chunk)]
        ahead = 2
        pending = [scores(*w) for w in work[:ahead]]
        for idx, (r0, n_keys) in enumerate(work):
            if idx + ahead < len(work):
                pending.append(scores(*work[idx + ahead]))
            softmax_pv(r0, n_keys, pending.pop(0))
        w = o_scr[0:tq, :] - lam * o_scr[tq:2 * tq, :]
        o_ref[0] = (_rms_rows(w, sg_ref[...]) * (1.0 - lam_init)).astype(BF16)

    for blk in range(k_ref.shape[1] // tq):
        pl.when(i == blk)(partial(attend, blk))


def _prompt_attn(q, kb, vb, lq1, lk1, lq2, lk2, sg, *, lam_init, tq):
    B, S, _ = q.shape
    grid = (B, ATT_HEADS, S // tq)
    vec = lambda a: pl.BlockSpec(a.shape, lambda b, h, i: (0, 0))
    return pl.pallas_call(
        partial(_prompt_attn_kernel, lam_init=lam_init, chunk=256),
        grid=grid,
        in_specs=[pl.BlockSpec((1, tq, ATT_DV), lambda b, h, i: (b, i, h)),
                  pl.BlockSpec((1, S, ATT_DV), lambda b, h, i: (b, 0, h)),
                  pl.BlockSpec((1, S, ATT_DV), lambda b, h, i: (b, 0, h)),
                  vec(lq1), vec(lk1), vec(lq2), vec(lk2), vec(sg)],
        out_specs=pl.BlockSpec((1, tq, ATT_DV), lambda b, h, i: (b, i, h)),
        out_shape=jax.ShapeDtypeStruct((B, S, ATT_W), BF16),
        scratch_shapes=[pltpu.VMEM((2 * tq, ATT_DV), F32)],
        compiler_params=pltpu.CompilerParams(
            dimension_semantics=("arbitrary", "arbitrary", "arbitrary"), vmem_limit_bytes=V7X_VMEM_LIMIT),
        name="prompt_attn",
    )(q, kb, vb, lq1, lk1, lq2, lk2, sg)


def _self_attention_init(qx, kn_ref, vn_ref, m_scr, l_scr, acc_scr, n_tok):
    rows = qx.shape[0]
    qf = qx.astype(F32)
    tok = _iota_mod((rows, 1), 0, rows // 2) >> (ATT_HEADS.bit_length() - 1)
    ss = []
    for j in range(n_tok):
        sj = jnp.sum(qf * kn_ref[0, j], axis=-1, keepdims=True)
        ss.append(jnp.where(tok >= j, sj, -jnp.inf))
    m0 = ss[0]
    for sj in ss[1:]:
        m0 = jnp.maximum(m0, sj)
    l0 = jnp.zeros((rows, 1), F32)
    a0 = jnp.zeros((rows, ATT_DV), F32)
    for j in range(n_tok):
        pj = jnp.exp2(ss[j] - m0)
        l0 = l0 + pj
        a0 = a0 + pj * vn_ref[0, j]
    m_scr[...] = jnp.broadcast_to(m0, (rows, LANES))
    l_scr[...] = jnp.where(lax.broadcasted_iota(jnp.int32, (rows, LANES), 1) == 0, l0, 0.0)
    acc_scr[...] = a0


def _attend_pages(qx, k_pages, v_pages, m_scr, l_scr, acc_scr):
    rows = qx.shape[0]
    own = _iota_mod((rows, LANES), 1, ATT_HEADS) == _iota_mod((rows, LANES), 0, ATT_HEADS)
    page_tiles = []
    for kp in k_pages:
        s = _nt_dot(qx, kp.astype(BF16))
        page_tiles.append([jnp.where(own, t, -jnp.inf) for t in _lane_tiles(s)])
    m_tile = None
    for tiles in page_tiles:
        for t in tiles:
            m_tile = t if m_tile is None else jnp.maximum(m_tile, t)
    m_prev = m_scr[...]
    m_new = jnp.maximum(m_prev, jnp.broadcast_to(jnp.max(m_tile, axis=-1, keepdims=True), (rows, LANES)))
    alpha = jnp.exp2(m_prev - m_new)
    l_tile = alpha * l_scr[...]
    acc = alpha * acc_scr[...]
    for tiles, vp in zip(page_tiles, v_pages):
        ps = [jnp.exp2(t - m_new) for t in tiles]
        for p in ps:
            l_tile = l_tile + p
        acc = acc + jnp.dot(jnp.concatenate(ps, axis=-1).astype(BF16), vp.astype(BF16),
                            preferred_element_type=F32)
    m_scr[...] = m_new
    l_scr[...] = l_tile
    acc_scr[...] = acc


def _sample_attention_output(lam, sg, l_scr, acc_scr, lam_init):
    half = acc_scr.shape[0] // 2
    o = acc_scr[...] / jnp.sum(l_scr[...], axis=-1, keepdims=True)
    w = o[:half] - lam * o[half:]
    return _rms_rows(w, sg) * (1.0 - lam_init)


def _ffn_tail(x_ref, att_ref, conv_ref, gm_ref, wo_ref, fg_ref, wg_ref, wu_ref, wd_ref, y_ref, act_scr, side_work=()):
    mix = (jnp.dot(att_ref[...], wo_ref[0:ATT_W, :], preferred_element_type=F32)
           + jnp.dot(conv_ref[...], wo_ref[ATT_W:ATT_W + CONV_W, :], preferred_element_type=F32)
           + jnp.dot(gm_ref[...], wo_ref[ATT_W + CONV_W:, :], preferred_element_type=F32))
    x1 = x_ref[...] + mix
    h = _rms_rows(x1, fg_ref[...]).astype(BF16)
    chunk_starts = range(0, wg_ref.shape[1], FF_CHUNK)
    assert len(side_work) <= len(chunk_starts)
    for n, c in enumerate(chunk_starts):
        gate = jnp.dot(h, wg_ref[:, c:c + FF_CHUNK], preferred_element_type=F32)
        up = jnp.dot(h, wu_ref[:, c:c + FF_CHUNK], preferred_element_type=F32)
        act_scr[:, c:c + FF_CHUNK] = (_silu(gate) * up).astype(BF16)
        if n < len(side_work):
            side_work[n]()
    y_ref[...] = x1 + jnp.dot(act_scr[...], wd_ref[...], preferred_element_type=F32)


def _tail_kernel(x_ref, att_ref, conv_ref, gm_ref, wo_ref, fg_ref, wg_ref, wu_ref, wd_ref, y_ref, act_scr):
    _ffn_tail(x_ref, att_ref, conv_ref, gm_ref, wo_ref, fg_ref, wg_ref, wu_ref, wd_ref, y_ref, act_scr)


def _tail(x, att, conv, gm, wo, fg, wg, wu, wd, *, tm):
    n, d = x.shape
    d_ff = wg.shape[1]
    assert d_ff % FF_CHUNK == 0
    row = lambda w: pl.BlockSpec((tm, w), lambda i: (i, 0))
    return pl.pallas_call(
        _tail_kernel,
        grid=(n // tm,),
        in_specs=[row(d), row(ATT_W), row(CONV_W), row(GM_W)]
                 + [_const_spec(a.shape) for a in (wo, fg, wg, wu, wd)],
        out_specs=row(d),
        out_shape=jax.ShapeDtypeStruct((n, d), F32),
        scratch_shapes=[pltpu.VMEM((tm, d_ff), BF16)],
        compiler_params=pltpu.CompilerParams(
            dimension_semantics=("arbitrary",), vmem_limit_bytes=V7X_VMEM_LIMIT),
        name="tail",
    )(x, att, conv, gm, wo, fg, wg, wu, wd)


PAGE_GROUP = 8
PAGE_SLOTS = 3


def _tail_attn_kernel(pt_ref, x_ref, att_ref, conv_ref, gm_ref, wo_ref, fg_ref, wg_ref, wu_ref, wd_ref,
                      qx_ref, kn_ref, vn_ref, lq1_ref, lk1_ref, lq2_ref, lk2_ref, sg_ref, ck_hbm, cv_hbm,
                      y_ref, o_ref,
                      act_scr, kbuf, vbuf, sems, m_scr, l_scr, acc_scr, *,
                      layer, lam_init, n_tok, groups_per_step, steps_per_batch):
    s = pl.program_id(0)
    n_slots, group = kbuf.shape[0], kbuf.shape[1]
    ahead = n_slots - 1
    n_groups = pl.num_programs(0) * groups_per_step

    def page_copies(gi):
        slot = lax.rem(gi, n_slots)
        copies = []
        for i in range(group):
            page = pt_ref[gi * group + i]
            copies.append(pltpu.make_async_copy(ck_hbm.at[layer, page], kbuf.at[slot, i], sems.at[0, slot]))
            copies.append(pltpu.make_async_copy(cv_hbm.at[layer, page], vbuf.at[slot, i], sems.at[1, slot]))
        return copies

    @pl.when(s == 0)
    def _():
        for gi in range(ahead):
            for cp in page_copies(gi):
                cp.start()

    qx = qx_ref[0]

    @pl.when(lax.rem(s, steps_per_batch) == 0)
    def _():
        _self_attention_init(qx, kn_ref, vn_ref, m_scr, l_scr, acc_scr, n_tok)

    def attend_group(g):
        gi = s * groups_per_step + g
        for cp in page_copies(gi):
            cp.wait()
        for cp in page_copies(gi + ahead):
            cp.start()
        slot = lax.rem(gi, n_slots)
        _attend_pages(qx, [kbuf[slot, i] for i in range(group)], [vbuf[slot, i] for i in range(group)],
                      m_scr, l_scr, acc_scr)

    _ffn_tail(x_ref, att_ref, conv_ref, gm_ref, wo_ref, fg_ref, wg_ref, wu_ref, wd_ref, y_ref, act_scr,
              side_work=[partial(attend_group, g) for g in range(groups_per_step)])

    @pl.when(lax.rem(s, steps_per_batch) == steps_per_batch - 1)
    def _():
        lam = _diff_lambda(lq1_ref[...], lk1_ref[...], lq2_ref[...], lk2_ref[...], lam_init)
        o_ref[0] = _sample_attention_output(lam, sg_ref[...], l_scr, acc_scr, lam_init)

    @pl.when(s == pl.num_programs(0) - 1)
    def _():
        for gi in range(ahead):
            for cp in page_copies(n_groups + gi):
                cp.wait()


def _tail_attn(x, att, conv, gm, wo, fg, wg, wu, wd, page_table, qx, kn, vn, lq1, lk1, lq2, lk2, sg,
               cache_k, cache_v, *, tm, layer, lam_init, n_tok):
    n, d = x.shape
    d_ff = wg.shape[1]
    Bd, rows, _ = qx.shape
    n_pages = page_table.shape[1]
    page_rows = cache_k.shape[2]
    n_steps = n // tm
    assert d_ff % FF_CHUNK == 0 and n_steps % Bd == 0
    steps_per_batch = n_steps // Bd
    assert n_pages % (steps_per_batch * PAGE_GROUP) == 0
    groups_per_step = n_pages // (steps_per_batch * PAGE_GROUP)
    assert groups_per_step <= d_ff // FF_CHUNK
    pt_flat = jnp.concatenate([page_table.reshape(-1),
                               jnp.zeros(((PAGE_SLOTS - 1) * PAGE_GROUP,), page_table.dtype)])
    row = lambda w: pl.BlockSpec((tm, w), lambda i, pt: (i, 0))
    vec = lambda a: pl.BlockSpec(a.shape, lambda i, pt: (0, 0))
    per_b = lambda a: pl.BlockSpec((1,) + a.shape[1:], lambda i, pt: (i // steps_per_batch,) + (0,) * (a.ndim - 1))
    hbm = pl.BlockSpec(memory_space=pl.ANY)
    return pl.pallas_call(
        partial(_tail_attn_kernel, layer=layer, lam_init=lam_init, n_tok=n_tok,
                groups_per_step=groups_per_step, steps_per_batch=steps_per_batch),
        grid_spec=pltpu.PrefetchScalarGridSpec(
            num_scalar_prefetch=1, grid=(n_steps,),
            in_specs=[row(d), row(ATT_W), row(CONV_W), row(GM_W)]
                     + [_const_spec(a.shape) for a in (wo, fg, wg, wu, wd)]
                     + [per_b(qx), per_b(kn), per_b(vn), vec(lq1), vec(lk1), vec(lq2), vec(lk2), vec(sg), hbm, hbm],
            out_specs=(row(d), pl.BlockSpec((1, rows // 2, ATT_DV), lambda i, pt: (i // steps_per_batch, 0, 0))),
            scratch_shapes=[pltpu.VMEM((tm, d_ff), BF16),
                            pltpu.VMEM((PAGE_SLOTS, PAGE_GROUP, page_rows, ATT_DV), F32),
                            pltpu.VMEM((PAGE_SLOTS, PAGE_GROUP, page_rows, ATT_DV), F32),
                            pltpu.SemaphoreType.DMA((2, PAGE_SLOTS)),
                            pltpu.VMEM((rows, LANES), F32), pltpu.VMEM((rows, LANES), F32),
                            pltpu.VMEM((rows, ATT_DV), F32)]),
        out_shape=(jax.ShapeDtypeStruct((n, d), F32), jax.ShapeDtypeStruct((Bd, rows // 2, ATT_DV), F32)),
        compiler_params=pltpu.CompilerParams(
            dimension_semantics=("arbitrary",), vmem_limit_bytes=V7X_VMEM_LIMIT),
        name="tail_attn",
    )(pt_flat, x, att, conv, gm, wo, fg, wg, wu, wd, qx, kn, vn, lq1, lk1, lq2, lk2, sg, cache_k, cache_v)


def kernel(x_prompt, x_sample, cache_k, cache_v, state_conv, page_table, mix_norm_g, w_in, q_norm_g, k_norm_g,
           lam_q1, lam_k1, lam_q2, lam_k2, subln_g, conv_w, conv_b, conv_norm_g, conv_norm_b, gm_norm_g,
           gm_norm_b, gm_ws, gm_bs, w_out, ffn_norm_g, w_gate, w_up, w_down):
    B, S, D = x_prompt.shape
    Bd, T, _ = x_sample.shape
    depth = w_in.shape[0]
    n_phys, page = cache_k.shape[1], cache_k.shape[2]
    assert cache_k.shape[3:] == (ATT_HEADS, 2 * ATT_DH) and cache_v.shape[3:] == (ATT_HEADS, ATT_DV)
    assert w_in.shape[-1] == _END and conv_w.shape[1:] == (CONV_K, CONV_W)
    assert gm_ws.shape[1:] == (GM_HEADS, CHUNK, CHUNK) and S % CHUNK == 0 and T <= CHUNK

    ck = cache_k.reshape(depth, n_phys, page * ATT_HEADS, 2 * ATT_DH)
    cv = cache_v.reshape(depth, n_phys, page * ATT_HEADS, ATT_DV)
    row = lambda a: a.reshape(1, -1)

    xp = x_prompt
    xs = jnp.swapaxes(x_sample, 0, 1).reshape(T * Bd, D)

    kv_all = None
    c_p, k_s, v_s, c_s, g_s = [], [], [], [], []
    for l in range(depth):
        lam_init = 0.8 - 0.6 * math.exp(-0.3 * l)
        w_in_b = w_in[l].astype(BF16)
        qg = row(jnp.tile(q_norm_g[l], QK_W // ATT_DH))
        kg = row(jnp.tile(k_norm_g[l], QK_W // ATT_DH))
        lam_vecs = (row(lam_q1[l]), row(lam_k1[l]), row(lam_q2[l]), row(lam_k2[l]))
        sg = row(subln_g[l])
        mix_args = (row(mix_norm_g[l]), w_in_b, qg, kg, conv_w[l], row(conv_b[l]), row(conv_norm_g[l]),
                    row(conv_norm_b[l]), row(gm_norm_g[l]), row(gm_norm_b[l]))
        tail_w = (w_out[l].astype(BF16), row(ffn_norm_g[l]), w_gate[l].astype(BF16), w_up[l].astype(BF16),
                  w_down[l].astype(BF16))

        ws_stack = gm_ws[l].reshape(GM_HEADS * CHUNK, CHUNK)
        bs_rows = jnp.repeat(gm_bs[l].T, GM_DH, axis=1)
        q, k_all, kb, v_all, vb, conv_p, gm_p, cst_p = _prompt_mix(
            xp, *mix_args, ws_stack, bs_rows, ts=512, layer=l, depth=depth, kv_all=kv_all)
        kv_all = (k_all, v_all)
        att_p = _prompt_attn(q, kb, vb, *lam_vecs, sg, lam_init=lam_init, tq=512)

        st = jnp.swapaxes(state_conv[l], 0, 1).reshape((CONV_K - 1) * Bd, CONV_W)
        wl = jnp.repeat(jnp.transpose(gm_ws[l][:, :T, :T], (1, 2, 0)).reshape(T * T, GM_HEADS), GM_DH, axis=1)
        bl = jnp.repeat(gm_bs[l][:, :T].T, GM_DH, axis=1)
        qs, ks, vs, conv_s, gm_s, cst, vrows = _sample_mix(xs, st, *mix_args, wl, bl, n_tok=T, n_batch=Bd)
        tb = lambda a: jnp.swapaxes(a.reshape(T, Bd, -1), 0, 1)
        k_bt = tb(ks).reshape(Bd, T, ATT_HEADS, 2 * ATT_DH)
        v_bt = tb(vs).reshape(Bd, T, ATT_HEADS, ATT_DV)
        qx = _split_maps(tb(qs).reshape(Bd, T * ATT_HEADS, 2 * ATT_DH)).astype(BF16)
        reps = 2 * T

        xp, att_s = _tail_attn(
            xp.reshape(B * S, D), att_p.reshape(B * S, ATT_W), conv_p.reshape(B * S, CONV_W),
            gm_p.reshape(B * S, GM_W), *tail_w, page_table, qx, jnp.tile(k_bt, (1, 1, reps, 1)),
            jnp.tile(v_bt, (1, 1, reps, 1)), *lam_vecs, sg, ck, cv, tm=512, layer=l, lam_init=lam_init, n_tok=T)
        xp = xp.reshape(B, S, D)
        att_s = jnp.swapaxes(att_s.reshape(Bd, T, ATT_W), 0, 1).reshape(T * Bd, ATT_W).astype(BF16)
        xs = _tail(xs, att_s, conv_s, gm_s, *tail_w, tm=T * Bd)
        c_p.append(cst_p)
        k_s.append(k_bt)
        v_s.append(v_bt)
        c_s.append(jnp.swapaxes(cst.reshape(CONV_K - 1, Bd, CONV_W), 0, 1))
        g_s.append(tb(vrows))

    ys = jnp.swapaxes(xs.reshape(T, Bd, D), 0, 1)
    k_all, v_all = kv_all
    return (xp, ys, k_all.reshape(depth, B, S, ATT_HEADS, 2 * ATT_DH), v_all.reshape(depth, B, S, ATT_HEADS, ATT_DV),
            jnp.stack(c_p),
            jnp.stack(k_s), jnp.stack(v_s), jnp.stack(c_s), jnp.stack(g_s))
```

```python
import math
from functools import partial

import jax
import jax.numpy as jnp
from jax import lax
from jax.experimental import pallas as pl
from jax.experimental.pallas import tpu as pltpu

F32 = jnp.float32
BF16 = jnp.bfloat16

EPS = 1e-6
ATT_HEADS = 4
ATT_DH = 64
ATT_DV = 2 * ATT_DH
ATT_W = ATT_HEADS * ATT_DV
QK_W = ATT_HEADS * 2 * ATT_DH
CONV_W = 256
CONV_K = 31
GM_HEADS = 4
GM_DH = 64
GM_W = GM_HEADS * GM_DH
CHUNK = 128
GROUP = 64
GROUP_TILE = 256
HALO = 32
LANES = 128
SUBLANES = 8
FF_CHUNK = 256

_Q0, _K0, _V0, _C0, _M0, _END = 0, QK_W, 2 * QK_W, 2 * QK_W + ATT_W, 2 * QK_W + ATT_W + 2 * CONV_W, \
    2 * QK_W + ATT_W + 2 * CONV_W + 2 * GM_W

Q_SCALE = ATT_DH ** -0.5 * math.log2(math.e)

V7X_VMEM_LIMIT = 56 * 1024 * 1024


def _iota_div(shape, axis, div):
    assert div & (div - 1) == 0
    return lax.broadcasted_iota(jnp.int32, shape, axis) >> (div.bit_length() - 1)


def _iota_mod(shape, axis, mod):
    assert mod & (mod - 1) == 0
    return lax.broadcasted_iota(jnp.int32, shape, axis) & (mod - 1)


def _group_matrix():
    r = _iota_div((GROUP_TILE, GROUP_TILE), 0, GROUP)
    c = _iota_div((GROUP_TILE, GROUP_TILE), 1, GROUP)
    return jnp.where(r == c, 1.0, 0.0).astype(BF16)


def _group_sum(x, gmat):
    return jnp.dot(x.astype(BF16), gmat, preferred_element_type=F32)


def _group_sum_split(x, gmat):
    hi = x.astype(BF16)
    lo = (x - hi.astype(F32)).astype(BF16)
    return (jnp.dot(hi, gmat, preferred_element_type=F32)
            + jnp.dot(lo, gmat, preferred_element_type=F32))


def _rms_rows(x, g):
    return x * lax.rsqrt(jnp.mean(x * x, axis=-1, keepdims=True) + EPS) * g


def _group_rms(z, g, gmat):
    parts = []
    for c in range(0, z.shape[-1], GROUP_TILE):
        zc = z[:, c:c + GROUP_TILE]
        ms = _group_sum(zc * zc, gmat) * (1.0 / GROUP)
        parts.append(zc * lax.rsqrt(ms + EPS))
    y = parts[0] if len(parts) == 1 else jnp.concatenate(parts, axis=-1)
    return y * g


def _group_ln(z, g, b, gmat):
    parts = []
    for c in range(0, z.shape[-1], GROUP_TILE):
        zc = z[:, c:c + GROUP_TILE]
        d = zc - _group_sum_split(zc, gmat) * (1.0 / GROUP)
        var = _group_sum(d * d, gmat) * (1.0 / GROUP)
        parts.append(d * lax.rsqrt(var + EPS))
    y = parts[0] if len(parts) == 1 else jnp.concatenate(parts, axis=-1)
    return y * g + b


def _sigmoid(x):
    return 1.0 / (1.0 + jnp.exp(-x))


def _silu(x):
    return x * _sigmoid(x)


def _gelu_tanh(x):
    return 0.5 * x * (1.0 + jnp.tanh(math.sqrt(2.0 / math.pi) * (x + 0.044715 * (x * x * x))))


def _project(xn, w_ref, lo, hi):
    return jnp.dot(xn, w_ref[:, lo:hi], preferred_element_type=F32)


def _diff_lambda(lq1, lk1, lq2, lk2, lam_init):
    a = jnp.sum(lq1 * lk1, axis=-1, keepdims=True)
    b = jnp.sum(lq2 * lk2, axis=-1, keepdims=True)
    return jnp.exp(a) - jnp.exp(b) + lam_init


def _split_maps(q):
    lane = lax.broadcasted_iota(jnp.int32, q.shape, q.ndim - 1)
    zero = jnp.zeros_like(q)
    return jnp.concatenate([jnp.where(lane < ATT_DH, q, zero), jnp.where(lane >= ATT_DH, q, zero)],
                           axis=q.ndim - 2)


def _nt_dot(a, b):
    return lax.dot_general(a, b, (((1,), (1,)), ((), ())), preferred_element_type=F32)


def _prompt_mix_kernel(x_ref, ng_ref, w_ref, qg_ref, kg_ref, cw_ref, cb_ref, cng_ref, cnb_ref,
                       gng_ref, gnb_ref, ws_ref, bs_ref,
                       q_out, k_out, kb_out, v_out, vb_out, conv_out, gm_out, cst_out,
                       glu_buf, shifted, *, rows_per_conv_chunk):
    j = pl.program_id(1)
    nj = pl.num_programs(1)
    ts = x_ref.shape[1]
    gmat = _group_matrix()

    @pl.when(j == 0)
    def _():
        glu_buf[0:HALO, :] = jnp.zeros((HALO, CONV_W), F32)

    xn = _rms_rows(x_ref[0], ng_ref[...]).astype(BF16)

    c = _project(xn, w_ref, _C0, _M0)
    glu_buf[HALO:HALO + ts, :] = c[:, :CONV_W] * _sigmoid(c[:, CONV_W:])
    first = HALO - (CONV_K - 1)
    n_shift = shifted.shape[1]
    for s in range(1, SUBLANES):
        shifted[s - 1] = glu_buf[s:s + n_shift, :]
    rc = rows_per_conv_chunk

    def conv_rows(r):
        acc = jnp.broadcast_to(cb_ref[...], (rc, CONV_W))
        for t in range(CONV_K):
            base, phase = (first + t) // SUBLANES * SUBLANES, (first + t) % SUBLANES
            src = glu_buf if phase == 0 else shifted.at[phase - 1]
            acc = acc + cw_ref[t:t + 1, :] * src[r + base:r + base + rc, :]
        return acc

    row_starts = list(range(0, ts, rc))
    third = -(-len(row_starts) // 3)
    ys = [conv_rows(r) for r in row_starts[:third]]

    q = _group_rms(_project(xn, w_ref, _Q0, _K0), qg_ref[...], gmat) * Q_SCALE
    q_out[0] = q.astype(BF16)
    ys += [conv_rows(r) for r in row_starts[third:2 * third]]

    k = _group_rms(_project(xn, w_ref, _K0, _V0), kg_ref[...], gmat)
    kb_out[0] = k.astype(BF16)
    ys += [conv_rows(r) for r in row_starts[2 * third:]]

    v = _project(xn, w_ref, _V0, _C0)
    vb_out[0] = v.astype(BF16)
    for slab in range(k_out.shape[0]):
        for h in range(ATT_HEADS):
            k_out[slab, pl.ds(h, ts, stride=ATT_HEADS), :] = k[:, h * ATT_DV:(h + 1) * ATT_DV]
            v_out[slab, pl.ds(h, ts, stride=ATT_HEADS), :] = v[:, h * ATT_DV:(h + 1) * ATT_DV]

    y = jnp.concatenate(ys, axis=0)
    conv_out[0] = _silu(_group_ln(y, cng_ref[...], cnb_ref[...], gmat)).astype(BF16)
    cst_out[0] = glu_buf[ts + first:ts + HALO, :]
    glu_buf[0:HALO, :] = glu_buf[ts:ts + HALO, :]

    m = _gelu_tanh(_project(xn, w_ref, _M0, _END))
    u = m[:, :GM_W]
    vln = _group_ln(m[:, GM_W:], gng_ref[...], gnb_ref[...], gmat).astype(BF16)
    wr = _iota_mod((GM_HEADS * CHUNK, CHUNK), 0, CHUNK)
    wc = lax.broadcasted_iota(jnp.int32, (GM_HEADS * CHUNK, CHUNK), 1)
    w_tril = jnp.where(wc <= wr, ws_ref[...], 0.0).astype(BF16)
    lane_head = _iota_div((CHUNK, GM_W), 1, GM_DH)
    outs = []
    for r in range(0, ts, CHUNK):
        sv = jnp.dot(w_tril, vln[r:r + CHUNK, :], preferred_element_type=F32)
        s = sv[0:CHUNK, :]
        for h in range(1, GM_HEADS):
            s = jnp.where(lane_head == h, sv[h * CHUNK:(h + 1) * CHUNK, :], s)
        outs.append(u[r:r + CHUNK, :] * (s + bs_ref[...]))
    gm_out[0] = jnp.concatenate(outs, axis=0).astype(BF16)


def _const_spec(shape):
    nd = len(shape)
    return pl.BlockSpec(shape, lambda *_: (0,) * nd, pipeline_mode=pl.Buffered(1))


def _prompt_mix_kernel_with_kv(*refs, n_in, rows_per_conv_chunk):
    _prompt_mix_kernel(*refs[:n_in], *refs[n_in + 2:], rows_per_conv_chunk=rows_per_conv_chunk)


def _prompt_mix(x, ng, w_in, qg, kg, cw, cb, cng, cnb, gng, gnb, ws, bs, *, ts, layer, depth, kv_all=None):
    B, S, D = x.shape
    grid = (B, S // ts)
    row = lambda w: pl.BlockSpec((1, ts, w), lambda b, j: (b, j, 0))
    if kv_all is None:
        by_head = pl.BlockSpec((depth, None, ts * ATT_HEADS, ATT_DV), lambda b, j: (0, b, j, 0))
    else:
        by_head = pl.BlockSpec((1, None, ts * ATT_HEADS, ATT_DV), lambda b, j: (layer, b, j, 0))
    inputs = (x, ng, w_in, qg, kg, cw, cb, cng, cnb, gng, gnb, ws, bs)
    in_specs = [row(D)] + [_const_spec(a.shape) for a in inputs[1:]]
    body = partial(_prompt_mix_kernel, rows_per_conv_chunk=64)
    aliases = {}
    if kv_all is not None:
        body = partial(_prompt_mix_kernel_with_kv, n_in=len(inputs), rows_per_conv_chunk=64)
        aliases = {len(inputs): 1, len(inputs) + 1: 3}
        in_specs = in_specs + [pl.BlockSpec(memory_space=pl.ANY)] * 2
        inputs = inputs + tuple(kv_all)
    outs = (
        jax.ShapeDtypeStruct((B, S, QK_W), BF16),
        jax.ShapeDtypeStruct((depth, B, S * ATT_HEADS, 2 * ATT_DH), F32),
        jax.ShapeDtypeStruct((B, S, QK_W), BF16),
        jax.ShapeDtypeStruct((depth, B, S * ATT_HEADS, ATT_DV), F32),
        jax.ShapeDtypeStruct((B, S, ATT_W), BF16),
        jax.ShapeDtypeStruct((B, S, CONV_W), BF16),
        jax.ShapeDtypeStruct((B, S, GM_W), BF16),
        jax.ShapeDtypeStruct((B, CONV_K - 1, CONV_W), F32),
    )
    return pl.pallas_call(
        body,
        grid=grid,
        in_specs=in_specs,
        out_specs=(row(QK_W), by_head, row(QK_W), by_head, row(ATT_W), row(CONV_W), row(GM_W),
                   pl.BlockSpec((1, CONV_K - 1, CONV_W), lambda b, j: (b, 0, 0))),
        out_shape=outs,
        input_output_aliases=aliases,
        scratch_shapes=[pltpu.VMEM((HALO + ts, CONV_W), F32),
                        pltpu.VMEM((SUBLANES - 1, HALO + ts - SUBLANES, CONV_W), F32)],
        compiler_params=pltpu.CompilerParams(
            dimension_semantics=("arbitrary", "arbitrary"), vmem_limit_bytes=V7X_VMEM_LIMIT),
        name="prompt_mix",
    )(*inputs)


def _sample_mix_kernel(x_ref, st_ref, ng_ref, w_ref, qg_ref, kg_ref, cw_ref, cb_ref, cng_ref, cnb_ref,
                       gng_ref, gnb_ref, wl_ref, bl_ref,
                       q_out, k_out, v_out, conv_out, gm_out, cst_out, vrow_out, buf, *, n_tok, n_batch):
    rows = n_tok * n_batch
    hist = (CONV_K - 1) * n_batch
    gmat = _group_matrix()
    xn = _rms_rows(x_ref[...], ng_ref[...]).astype(BF16)

    q_out[...] = _group_rms(_project(xn, w_ref, _Q0, _K0), qg_ref[...], gmat) * Q_SCALE
    k_out[...] = _group_rms(_project(xn, w_ref, _K0, _V0), kg_ref[...], gmat)
    v_out[...] = _project(xn, w_ref, _V0, _C0)

    c = _project(xn, w_ref, _C0, _M0)
    buf[0:hist, :] = st_ref[...]
    buf[hist:hist + rows, :] = c[:, :CONV_W] * _sigmoid(c[:, CONV_W:])
    acc = jnp.broadcast_to(cb_ref[...], (rows, CONV_W))
    for t in range(CONV_K):
        acc = acc + cw_ref[t:t + 1, :] * buf[t * n_batch:t * n_batch + rows, :]
    conv_out[...] = _silu(_group_ln(acc, cng_ref[...], cnb_ref[...], gmat)).astype(BF16)
    cst_out[...] = buf[rows:rows + hist, :]

    m = _gelu_tanh(_project(xn, w_ref, _M0, _END))
    u = m[:, :GM_W]
    vln = _group_ln(m[:, GM_W:], gng_ref[...], gnb_ref[...], gmat)
    vrow_out[...] = vln
    outs = []
    for i in range(n_tok):
        s = jnp.broadcast_to(bl_ref[i:i + 1, :], (n_batch, GM_W))
        for jj in range(i + 1):
            s = s + wl_ref[i * n_tok + jj:i * n_tok + jj + 1, :] * vln[jj * n_batch:(jj + 1) * n_batch, :]
        outs.append(u[i * n_batch:(i + 1) * n_batch, :] * s)
    gm_out[...] = jnp.concatenate(outs, axis=0).astype(BF16)


def _sample_mix(x, st, ng, w_in, qg, kg, cw, cb, cng, cnb, gng, gnb, wl, bl, *, n_tok, n_batch):
    rows = n_tok * n_batch
    hist = (CONV_K - 1) * n_batch
    outs = (
        jax.ShapeDtypeStruct((rows, QK_W), F32),
        jax.ShapeDtypeStruct((rows, QK_W), F32),
        jax.ShapeDtypeStruct((rows, ATT_W), F32),
        jax.ShapeDtypeStruct((rows, CONV_W), BF16),
        jax.ShapeDtypeStruct((rows, GM_W), BF16),
        jax.ShapeDtypeStruct((hist, CONV_W), F32),
        jax.ShapeDtypeStruct((rows, GM_W), F32),
    )
    return pl.pallas_call(
        partial(_sample_mix_kernel, n_tok=n_tok, n_batch=n_batch),
        out_shape=outs,
        scratch_shapes=[pltpu.VMEM((hist + rows, CONV_W), F32)],
        compiler_params=pltpu.CompilerParams(vmem_limit_bytes=V7X_VMEM_LIMIT),
        name="sample_mix",
    )(x, st, ng, w_in, qg, kg, cw, cb, cng, cnb, gng, gnb, wl, bl)


def _lane_tiles(x):
    return [x[:, t:t + LANES] for t in range(0, x.shape[-1], LANES)]


def _prompt_attn_kernel(q_ref, k_ref, v_ref, lq1_ref, lk1_ref, lq2_ref, lk2_ref, sg_ref, o_ref, o_scr, *,
                        lam_init, chunk):
    i = pl.program_id(2)
    tq = q_ref.shape[1]
    q2 = _split_maps(q_ref[0])
    lam = _diff_lambda(lq1_ref[...], lk1_ref[...], lq2_ref[...], lk2_ref[...], lam_init)

    def scores(r0, n_keys):
        s = _nt_dot(q2[r0:r0 + chunk], k_ref[0, 0:n_keys, :])
        tiles = _lane_tiles(s)
        n_diag = chunk // LANES
        for t in range(n_diag):
            rpos = lax.broadcasted_iota(jnp.int32, (chunk, LANES), 0)
            cpos = lax.broadcasted_iota(jnp.int32, (chunk, LANES), 1) + t * LANES
            tiles[t - n_diag] = jnp.where(cpos <= rpos, tiles[t - n_diag], -jnp.inf)
        return tiles

    def softmax_pv(r0, n_keys, tiles):
        m_tile = tiles[0]
        for t in tiles[1:]:
            m_tile = jnp.maximum(m_tile, t)
        m = jnp.broadcast_to(jnp.max(m_tile, axis=-1, keepdims=True), (chunk, LANES))
        ps = [jnp.exp2(t - m) for t in tiles]
        l_tile = ps[0]
        for p in ps[1:]:
            l_tile = l_tile + p
        acc = jnp.dot(jnp.concatenate(ps, axis=-1).astype(BF16), v_ref[0, 0:n_keys, :],
                      preferred_element_type=F32)
        o_scr[r0:r0 + chunk, :] = acc / jnp.sum(l_tile, axis=-1, keepdims=True)

    def attend(blk):
        work = [(r0, ((blk * tq + r0 % tq) // chunk + 1) * chunk) for r0 in range(0, 2 * tq, chunk)]
        ahead = 2
        pending = [scores(*w) for w in work[:ahead]]
        for idx, (r0, n_keys) in enumerate(work):
            if idx + ahead < len(work):
                pending.append(scores(*work[idx + ahead]))
            softmax_pv(r0, n_keys, pending.pop(0))
        w = o_scr[0:tq, :] - lam * o_scr[tq:2 * tq, :]
        o_ref[0] = (_rms_rows(w, sg_ref[...]) * (1.0 - lam_init)).astype(BF16)

    for blk in range(k_ref.shape[1] // tq):
        pl.when(i == blk)(partial(attend, blk))


def _prompt_attn(q, kb, vb, lq1, lk1, lq2, lk2, sg, *, lam_init, tq):
    B, S, _ = q.shape
    grid = (B, ATT_HEADS, S // tq)
    vec = lambda a: pl.BlockSpec(a.shape, lambda b, h, i: (0, 0))
    return pl.pallas_call(
        partial(_prompt_attn_kernel, lam_init=lam_init, chunk=256),
        grid=grid,
        in_specs=[pl.BlockSpec((1, tq, ATT_DV), lambda b, h, i: (b, i, h)),
                  pl.BlockSpec((1, S, ATT_DV), lambda b, h, i: (b, 0, h)),
                  pl.BlockSpec((1, S, ATT_DV), lambda b, h, i: (b, 0, h)),
                  vec(lq1), vec(lk1), vec(lq2), vec(lk2), vec(sg)],
        out_specs=pl.BlockSpec((1, tq, ATT_DV), lambda b, h, i: (b, i, h)),
        out_shape=jax.ShapeDtypeStruct((B, S, ATT_W), BF16),
        scratch_shapes=[pltpu.VMEM((2 * tq, ATT_DV), F32)],
        compiler_params=pltpu.CompilerParams(
            dimension_semantics=("arbitrary", "arbitrary", "arbitrary"), vmem_limit_bytes=V7X_VMEM_LIMIT),
        name="prompt_attn",
    )(q, kb, vb, lq1, lk1, lq2, lk2, sg)


def _self_attention_init(qx, kn_ref, vn_ref, m_scr, l_scr, acc_scr, n_tok):
    rows = qx.shape[0]
    qf = qx.astype(F32)
    tok = _iota_mod((rows, 1), 0, rows // 2) >> (ATT_HEADS.bit_length() - 1)
    ss = []
    for j in range(n_tok):
        sj = jnp.sum(qf * kn_ref[0, j], axis=-1, keepdims=True)
        ss.append(jnp.where(tok >= j, sj, -jnp.inf))
    m0 = ss[0]
    for sj in ss[1:]:
        m0 = jnp.maximum(m0, sj)
    l0 = jnp.zeros((rows, 1), F32)
    a0 = jnp.zeros((rows, ATT_DV), F32)
    for j in range(n_tok):
        pj = jnp.exp2(ss[j] - m0)
        l0 = l0 + pj
        a0 = a0 + pj * vn_ref[0, j]
    m_scr[...] = jnp.broadcast_to(m0, (rows, LANES))
    l_scr[...] = jnp.where(lax.broadcasted_iota(jnp.int32, (rows, LANES), 1) == 0, l0, 0.0)
    acc_scr[...] = a0


def _page_scores(qx, k_pages):
    rows = qx.shape[0]
    own = _iota_mod((rows, LANES), 1, ATT_HEADS) == _iota_mod((rows, LANES), 0, ATT_HEADS)
    page_tiles = []
    for kp in k_pages:
        s = _nt_dot(qx, kp.astype(BF16))
        page_tiles.append([jnp.where(own, t, -jnp.inf) for t in _lane_tiles(s)])
    return page_tiles


def _attend_pages(page_tiles, v_pages, m_scr, l_scr, acc_scr):
    rows = acc_scr.shape[0]
    m_tile = None
    for tiles in page_tiles:
        for t in tiles:
            m_tile = t if m_tile is None else jnp.maximum(m_tile, t)
    m_prev = m_scr[...]
    m_new = jnp.maximum(m_prev, jnp.broadcast_to(jnp.max(m_tile, axis=-1, keepdims=True), (rows, LANES)))
    alpha = jnp.exp2(m_prev - m_new)
    l_tile = alpha * l_scr[...]
    acc = alpha * acc_scr[...]
    for tiles, vp in zip(page_tiles, v_pages):
        ps = [jnp.exp2(t - m_new) for t in tiles]
        for p in ps:
            l_tile = l_tile + p
        acc = acc + jnp.dot(jnp.concatenate(ps, axis=-1).astype(BF16), vp.astype(BF16),
                            preferred_element_type=F32)
    m_scr[...] = m_new
    l_scr[...] = l_tile
    acc_scr[...] = acc


def _sample_attention_output(lam, sg, l_scr, acc_scr, lam_init):
    half = acc_scr.shape[0] // 2
    o = acc_scr[...] / jnp.sum(l_scr[...], axis=-1, keepdims=True)
    w = o[:half] - lam * o[half:]
    return _rms_rows(w, sg) * (1.0 - lam_init)


def _ffn_tail(x_ref, att_ref, conv_ref, gm_ref, wo_ref, fg_ref, wg_ref, wu_ref, wd_ref, y_ref, act_scr, side_work=()):
    mix = (jnp.dot(att_ref[...], wo_ref[0:ATT_W, :], preferred_element_type=F32)
           + jnp.dot(conv_ref[...], wo_ref[ATT_W:ATT_W + CONV_W, :], preferred_element_type=F32)
           + jnp.dot(gm_ref[...], wo_ref[ATT_W + CONV_W:, :], preferred_element_type=F32))
    x1 = x_ref[...] + mix
    h = _rms_rows(x1, fg_ref[...]).astype(BF16)
    chunk_starts = range(0, wg_ref.shape[1], FF_CHUNK)
    assert len(side_work) <= len(chunk_starts)
    for n, c in enumerate(chunk_starts):
        gate = jnp.dot(h, wg_ref[:, c:c + FF_CHUNK], preferred_element_type=F32)
        up = jnp.dot(h, wu_ref[:, c:c + FF_CHUNK], preferred_element_type=F32)
        act_scr[:, c:c + FF_CHUNK] = (_silu(gate) * up).astype(BF16)
        if n < len(side_work):
            side_work[n]()
    y_ref[...] = x1 + jnp.dot(act_scr[...], wd_ref[...], preferred_element_type=F32)


def _tail_kernel(x_ref, att_ref, conv_ref, gm_ref, wo_ref, fg_ref, wg_ref, wu_ref, wd_ref, y_ref, act_scr):
    _ffn_tail(x_ref, att_ref, conv_ref, gm_ref, wo_ref, fg_ref, wg_ref, wu_ref, wd_ref, y_ref, act_scr)


def _tail(x, att, conv, gm, wo, fg, wg, wu, wd, *, tm):
    n, d = x.shape
    d_ff = wg.shape[1]
    assert d_ff % FF_CHUNK == 0
    row = lambda w: pl.BlockSpec((tm, w), lambda i: (i, 0))
    return pl.pallas_call(
        _tail_kernel,
        grid=(n // tm,),
        in_specs=[row(d), row(ATT_W), row(CONV_W), row(GM_W)]
                 + [_const_spec(a.shape) for a in (wo, fg, wg, wu, wd)],
        out_specs=row(d),
        out_shape=jax.ShapeDtypeStruct((n, d), F32),
        scratch_shapes=[pltpu.VMEM((tm, d_ff), BF16)],
        compiler_params=pltpu.CompilerParams(
            dimension_semantics=("arbitrary",), vmem_limit_bytes=V7X_VMEM_LIMIT),
        name="tail",
    )(x, att, conv, gm, wo, fg, wg, wu, wd)


PAGE_GROUP = 8
PAGE_SLOTS = 4
PAGE_LIVE = 2
PAGE_AHEAD = PAGE_SLOTS - PAGE_LIVE


def _tail_attn_kernel(pt_ref, x_ref, att_ref, conv_ref, gm_ref, wo_ref, fg_ref, wg_ref, wu_ref, wd_ref,
                      qx_ref, kn_ref, vn_ref, lq1_ref, lk1_ref, lq2_ref, lk2_ref, sg_ref, ck_hbm, cv_hbm,
                      y_ref, o_ref,
                      act_scr, kbuf, vbuf, sems, m_scr, l_scr, acc_scr, *,
                      layer, lam_init, n_tok, groups_per_step, steps_per_batch):
    s = pl.program_id(0)
    n_slots, group = kbuf.shape[0], kbuf.shape[1]
    ahead = PAGE_AHEAD

    def slot_of(g):
        return lax.rem(s * groups_per_step + g, n_slots)

    def page_copies(g):
        gi = s * groups_per_step + g
        slot = slot_of(g)
        copies = []
        for i in range(group):
            page = pt_ref[gi * group + i]
            copies.append(pltpu.make_async_copy(ck_hbm.at[layer, page], kbuf.at[slot, i], sems.at[0, slot]))
            copies.append(pltpu.make_async_copy(cv_hbm.at[layer, page], vbuf.at[slot, i], sems.at[1, slot]))
        return copies

    @pl.when(s == 0)
    def _():
        for g in range(ahead):
            for cp in page_copies(g):
                cp.start()

    qx = qx_ref[0]

    @pl.when(lax.rem(s, steps_per_batch) == 0)
    def _():
        _self_attention_init(qx, kn_ref, vn_ref, m_scr, l_scr, acc_scr, n_tok)

    def group_scores(g):
        for cp in page_copies(g):
            cp.wait()
        for cp in page_copies(g + ahead):
            cp.start()
        slot = slot_of(g)
        return _page_scores(qx, [kbuf[slot, i] for i in range(group)])

    pending = [group_scores(0)]

    def attend_group(g):
        if g + 1 < groups_per_step:
            pending.append(group_scores(g + 1))
        slot = slot_of(g)
        _attend_pages(pending.pop(0), [vbuf[slot, i] for i in range(group)], m_scr, l_scr, acc_scr)

    _ffn_tail(x_ref, att_ref, conv_ref, gm_ref, wo_ref, fg_ref, wg_ref, wu_ref, wd_ref, y_ref, act_scr,
              side_work=[partial(attend_group, g) for g in range(groups_per_step)])

    @pl.when(lax.rem(s, steps_per_batch) == steps_per_batch - 1)
    def _():
        lam = _diff_lambda(lq1_ref[...], lk1_ref[...], lq2_ref[...], lk2_ref[...], lam_init)
        o_ref[0] = _sample_attention_output(lam, sg_ref[...], l_scr, acc_scr, lam_init)

    @pl.when(s == pl.num_programs(0) - 1)
    def _():
        for g in range(groups_per_step, groups_per_step + ahead):
            for cp in page_copies(g):
                cp.wait()


def _tail_attn(x, att, conv, gm, wo, fg, wg, wu, wd, page_table, qx, kn, vn, lq1, lk1, lq2, lk2, sg,
               cache_k, cache_v, *, tm, layer, lam_init, n_tok):
    n, d = x.shape
    d_ff = wg.shape[1]
    Bd, rows, _ = qx.shape
    n_pages = page_table.shape[1]
    page_rows = cache_k.shape[2]
    n_steps = n // tm
    assert d_ff % FF_CHUNK == 0 and n_steps % Bd == 0
    steps_per_batch = n_steps // Bd
    assert n_pages % (steps_per_batch * PAGE_GROUP) == 0
    groups_per_step = n_pages // (steps_per_batch * PAGE_GROUP)
    assert groups_per_step <= d_ff // FF_CHUNK
    pt_flat = jnp.concatenate([page_table.reshape(-1),
                               jnp.zeros((PAGE_AHEAD * PAGE_GROUP,), page_table.dtype)])
    row = lambda w: pl.BlockSpec((tm, w), lambda i, pt: (i, 0))
    vec = lambda a: pl.BlockSpec(a.shape, lambda i, pt: (0, 0))
    per_b = lambda a: pl.BlockSpec((1,) + a.shape[1:], lambda i, pt: (i // steps_per_batch,) + (0,) * (a.ndim - 1))
    hbm = pl.BlockSpec(memory_space=pl.ANY)
    return pl.pallas_call(
        partial(_tail_attn_kernel, layer=layer, lam_init=lam_init, n_tok=n_tok,
                groups_per_step=groups_per_step, steps_per_batch=steps_per_batch),
        grid_spec=pltpu.PrefetchScalarGridSpec(
            num_scalar_prefetch=1, grid=(n_steps,),
            in_specs=[row(d), row(ATT_W), row(CONV_W), row(GM_W)]
                     + [_const_spec(a.shape) for a in (wo, fg, wg, wu, wd)]
                     + [per_b(qx), per_b(kn), per_b(vn), vec(lq1), vec(lk1), vec(lq2), vec(lk2), vec(sg), hbm, hbm],
            out_specs=(row(d), pl.BlockSpec((1, rows // 2, ATT_DV), lambda i, pt: (i // steps_per_batch, 0, 0))),
            scratch_shapes=[pltpu.VMEM((tm, d_ff), BF16),
                            pltpu.VMEM((PAGE_SLOTS, PAGE_GROUP, page_rows, ATT_DV), F32),
                            pltpu.VMEM((PAGE_SLOTS, PAGE_GROUP, page_rows, ATT_DV), F32),
                            pltpu.SemaphoreType.DMA((2, PAGE_SLOTS)),
                            pltpu.VMEM((rows, LANES), F32), pltpu.VMEM((rows, LANES), F32),
                            pltpu.VMEM((rows, ATT_DV), F32)]),
        out_shape=(jax.ShapeDtypeStruct((n, d), F32), jax.ShapeDtypeStruct((Bd, rows // 2, ATT_DV), F32)),
        compiler_params=pltpu.CompilerParams(
            dimension_semantics=("arbitrary",), vmem_limit_bytes=V7X_VMEM_LIMIT),
        name="tail_attn",
    )(pt_flat, x, att, conv, gm, wo, fg, wg, wu, wd, qx, kn, vn, lq1, lk1, lq2, lk2, sg, cache_k, cache_v)


def kernel(x_prompt, x_sample, cache_k, cache_v, state_conv, page_table, mix_norm_g, w_in, q_norm_g, k_norm_g,
           lam_q1, lam_k1, lam_q2, lam_k2, subln_g, conv_w, conv_b, conv_norm_g, conv_norm_b, gm_norm_g,
           gm_norm_b, gm_ws, gm_bs, w_out, ffn_norm_g, w_gate, w_up, w_down):
    B, S, D = x_prompt.shape
    Bd, T, _ = x_sample.shape
    depth = w_in.shape[0]
    n_phys, page = cache_k.shape[1], cache_k.shape[2]
    assert cache_k.shape[3:] == (ATT_HEADS, 2 * ATT_DH) and cache_v.shape[3:] == (ATT_HEADS, ATT_DV)
    assert w_in.shape[-1] == _END and conv_w.shape[1:] == (CONV_K, CONV_W)
    assert gm_ws.shape[1:] == (GM_HEADS, CHUNK, CHUNK) and S % CHUNK == 0 and T <= CHUNK

    ck = cache_k.reshape(depth, n_phys, page * ATT_HEADS, 2 * ATT_DH)
    cv = cache_v.reshape(depth, n_phys, page * ATT_HEADS, ATT_DV)
    row = lambda a: a.reshape(1, -1)

    xp = x_prompt
    xs = jnp.swapaxes(x_sample, 0, 1).reshape(T * Bd, D)

    kv_all = None
    c_p, k_s, v_s, c_s, g_s = [], [], [], [], []
    for l in range(depth):
        lam_init = 0.8 - 0.6 * math.exp(-0.3 * l)
        w_in_b = w_in[l].astype(BF16)
        qg = row(jnp.tile(q_norm_g[l], QK_W // ATT_DH))
        kg = row(jnp.tile(k_norm_g[l], QK_W // ATT_DH))
        lam_vecs = (row(lam_q1[l]), row(lam_k1[l]), row(lam_q2[l]), row(lam_k2[l]))
        sg = row(subln_g[l])
        mix_args = (row(mix_norm_g[l]), w_in_b, qg, kg, conv_w[l], row(conv_b[l]), row(conv_norm_g[l]),
                    row(conv_norm_b[l]), row(gm_norm_g[l]), row(gm_norm_b[l]))
        tail_w = (w_out[l].astype(BF16), row(ffn_norm_g[l]), w_gate[l].astype(BF16), w_up[l].astype(BF16),
                  w_down[l].astype(BF16))

        ws_stack = gm_ws[l].reshape(GM_HEADS * CHUNK, CHUNK)
        bs_rows = jnp.repeat(gm_bs[l].T, GM_DH, axis=1)
        q, k_all, kb, v_all, vb, conv_p, gm_p, cst_p = _prompt_mix(
            xp, *mix_args, ws_stack, bs_rows, ts=512, layer=l, depth=depth, kv_all=kv_all)
        kv_all = (k_all, v_all)
        att_p = _prompt_attn(q, kb, vb, *lam_vecs, sg, lam_init=lam_init, tq=S)

        st = jnp.swapaxes(state_conv[l], 0, 1).reshape((CONV_K - 1) * Bd, CONV_W)
        wl = jnp.repeat(jnp.transpose(gm_ws[l][:, :T, :T], (1, 2, 0)).reshape(T * T, GM_HEADS), GM_DH, axis=1)
        bl = jnp.repeat(gm_bs[l][:, :T].T, GM_DH, axis=1)
        qs, ks, vs, conv_s, gm_s, cst, vrows = _sample_mix(xs, st, *mix_args, wl, bl, n_tok=T, n_batch=Bd)
        tb = lambda a: jnp.swapaxes(a.reshape(T, Bd, -1), 0, 1)
        k_bt = tb(ks).reshape(Bd, T, ATT_HEADS, 2 * ATT_DH)
        v_bt = tb(vs).reshape(Bd, T, ATT_HEADS, ATT_DV)
        qx = _split_maps(tb(qs).reshape(Bd, T * ATT_HEADS, 2 * ATT_DH)).astype(BF16)
        reps = 2 * T

        xp, att_s = _tail_attn(
            xp.reshape(B * S, D), att_p.reshape(B * S, ATT_W), conv_p.reshape(B * S, CONV_W),
            gm_p.reshape(B * S, GM_W), *tail_w, page_table, qx, jnp.tile(k_bt, (1, 1, reps, 1)),
            jnp.tile(v_bt, (1, 1, reps, 1)), *lam_vecs, sg, ck, cv, tm=512, layer=l, lam_init=lam_init, n_tok=T)
        xp = xp.reshape(B, S, D)
        att_s = jnp.swapaxes(att_s.reshape(Bd, T, ATT_W), 0, 1).reshape(T * Bd, ATT_W).astype(BF16)
        xs = _tail(xs, att_s, conv_s, gm_s, *tail_w, tm=T * Bd)
        c_p.append(cst_p)
        k_s.append(k_bt)
        v_s.append(v_bt)
        c_s.append(jnp.swapaxes(cst.reshape(CONV_K - 1, Bd, CONV_W), 0, 1))
        g_s.append(tb(vrows))

    ys = jnp.swapaxes(xs.reshape(T, Bd, D), 0, 1)
    k_all, v_all = kv_all
    return (xp, ys, k_all.reshape(depth, B, S, ATT_HEADS, 2 * ATT_DH), v_all.reshape(depth, B, S, ATT_HEADS, ATT_DV),
            jnp.stack(c_p),
            jnp.stack(k_s), jnp.stack(v_s), jnp.stack(c_s), jnp.stack(g_s))
```

```python
import math
from functools import partial

import jax
import jax.numpy as jnp
from jax import lax
from jax.experimental import pallas as pl
from jax.experimental.pallas import tpu as pltpu

F32 = jnp.float32
BF16 = jnp.bfloat16

EPS = 1e-6
ATT_HEADS = 4
ATT_DH = 64
ATT_DV = 2 * ATT_DH
ATT_W = ATT_HEADS * ATT_DV
QK_W = ATT_HEADS * 2 * ATT_DH
CONV_W = 256
CONV_K = 31
GM_HEADS = 4
GM_DH = 64
GM_W = GM_HEADS * GM_DH
CHUNK = 128
GROUP = 64
GROUP_TILE = 256
HALO = 32
LANES = 128
SUBLANES = 8
FF_CHUNK = 256

_Q0, _K0, _V0, _C0, _M0, _END = 0, QK_W, 2 * QK_W, 2 * QK_W + ATT_W, 2 * QK_W + ATT_W + 2 * CONV_W, \
    2 * QK_W + ATT_W + 2 * CONV_W + 2 * GM_W

Q_SCALE = ATT_DH ** -0.5 * math.log2(math.e)

V7X_VMEM_LIMIT = 56 * 1024 * 1024


def _iota_div(shape, axis, div):
    assert div & (div - 1) == 0
    return lax.broadcasted_iota(jnp.int32, shape, axis) >> (div.bit_length() - 1)


def _iota_mod(shape, axis, mod):
    assert mod & (mod - 1) == 0
    return lax.broadcasted_iota(jnp.int32, shape, axis) & (mod - 1)


def _group_matrix():
    r = _iota_div((GROUP_TILE, GROUP_TILE), 0, GROUP)
    c = _iota_div((GROUP_TILE, GROUP_TILE), 1, GROUP)
    return jnp.where(r == c, 1.0, 0.0).astype(BF16)


def _group_sum(x, gmat):
    return jnp.dot(x.astype(BF16), gmat, preferred_element_type=F32)


def _group_sum_split(x, gmat):
    hi = x.astype(BF16)
    lo = (x - hi.astype(F32)).astype(BF16)
    return (jnp.dot(hi, gmat, preferred_element_type=F32)
            + jnp.dot(lo, gmat, preferred_element_type=F32))


def _rms_rows(x, g):
    return x * lax.rsqrt(jnp.mean(x * x, axis=-1, keepdims=True) + EPS) * g


def _group_rms(z, g, gmat):
    parts = []
    for c in range(0, z.shape[-1], GROUP_TILE):
        zc = z[:, c:c + GROUP_TILE]
        ms = _group_sum(zc * zc, gmat) * (1.0 / GROUP)
        parts.append(zc * lax.rsqrt(ms + EPS))
    y = parts[0] if len(parts) == 1 else jnp.concatenate(parts, axis=-1)
    return y * g


def _group_ln(z, g, b, gmat):
    parts = []
    for c in range(0, z.shape[-1], GROUP_TILE):
        zc = z[:, c:c + GROUP_TILE]
        d = zc - _group_sum_split(zc, gmat) * (1.0 / GROUP)
        var = _group_sum(d * d, gmat) * (1.0 / GROUP)
        parts.append(d * lax.rsqrt(var + EPS))
    y = parts[0] if len(parts) == 1 else jnp.concatenate(parts, axis=-1)
    return y * g + b


def _sigmoid(x):
    return 1.0 / (1.0 + jnp.exp(-x))


def _silu(x):
    return x * _sigmoid(x)


def _gelu_tanh(x):
    return 0.5 * x * (1.0 + jnp.tanh(math.sqrt(2.0 / math.pi) * (x + 0.044715 * (x * x * x))))


def _project(xn, w_ref, lo, hi):
    return jnp.dot(xn, w_ref[:, lo:hi], preferred_element_type=F32)


def _diff_lambda(lq1, lk1, lq2, lk2, lam_init):
    a = jnp.sum(lq1 * lk1, axis=-1, keepdims=True)
    b = jnp.sum(lq2 * lk2, axis=-1, keepdims=True)
    return jnp.exp(a) - jnp.exp(b) + lam_init


def _split_maps(q):
    lane = lax.broadcasted_iota(jnp.int32, q.shape, q.ndim - 1)
    zero = jnp.zeros_like(q)
    return jnp.concatenate([jnp.where(lane < ATT_DH, q, zero), jnp.where(lane >= ATT_DH, q, zero)],
                           axis=q.ndim - 2)


def _nt_dot(a, b):
    return lax.dot_general(a, b, (((1,), (1,)), ((), ())), preferred_element_type=F32)


def _prompt_mix_kernel(x_ref, ng_ref, w_ref, qg_ref, kg_ref, cw_ref, cb_ref, cng_ref, cnb_ref,
                       gng_ref, gnb_ref, ws_ref, bs_ref,
                       q_out, k_out, kb_out, v_out, vb_out, conv_out, gm_out, cst_out,
                       glu_buf, shifted, *, rows_per_conv_chunk):
    j = pl.program_id(1)
    nj = pl.num_programs(1)
    ts = x_ref.shape[1]
    gmat = _group_matrix()

    @pl.when(j == 0)
    def _():
        glu_buf[0:HALO, :] = jnp.zeros((HALO, CONV_W), F32)

    xn = _rms_rows(x_ref[0], ng_ref[...]).astype(BF16)

    c = _project(xn, w_ref, _C0, _M0)
    glu_buf[HALO:HALO + ts, :] = c[:, :CONV_W] * _sigmoid(c[:, CONV_W:])
    first = HALO - (CONV_K - 1)
    n_shift = shifted.shape[1]
    for s in range(1, SUBLANES):
        shifted[s - 1] = glu_buf[s:s + n_shift, :]
    rc = rows_per_conv_chunk

    def conv_rows(r):
        acc = jnp.broadcast_to(cb_ref[...], (rc, CONV_W))
        for t in range(CONV_K):
            base, phase = (first + t) // SUBLANES * SUBLANES, (first + t) % SUBLANES
            src = glu_buf if phase == 0 else shifted.at[phase - 1]
            acc = acc + cw_ref[t:t + 1, :] * src[r + base:r + base + rc, :]
        return acc

    row_starts = list(range(0, ts, rc))
    third = -(-len(row_starts) // 3)
    ys = [conv_rows(r) for r in row_starts[:third]]

    q = _group_rms(_project(xn, w_ref, _Q0, _K0), qg_ref[...], gmat) * Q_SCALE
    q_out[0] = q.astype(BF16)
    ys += [conv_rows(r) for r in row_starts[third:2 * third]]

    k = _group_rms(_project(xn, w_ref, _K0, _V0), kg_ref[...], gmat)
    kb_out[0] = k.astype(BF16)
    ys += [conv_rows(r) for r in row_starts[2 * third:]]

    v = _project(xn, w_ref, _V0, _C0)
    vb_out[0] = v.astype(BF16)
    for slab in range(k_out.shape[0]):
        for h in range(ATT_HEADS):
            k_out[slab, pl.ds(h, ts, stride=ATT_HEADS), :] = k[:, h * ATT_DV:(h + 1) * ATT_DV]
            v_out[slab, pl.ds(h, ts, stride=ATT_HEADS), :] = v[:, h * ATT_DV:(h + 1) * ATT_DV]

    y = jnp.concatenate(ys, axis=0)
    conv_out[0] = _silu(_group_ln(y, cng_ref[...], cnb_ref[...], gmat)).astype(BF16)
    cst_out[0] = glu_buf[ts + first:ts + HALO, :]
    glu_buf[0:HALO, :] = glu_buf[ts:ts + HALO, :]

    m = _gelu_tanh(_project(xn, w_ref, _M0, _END))
    u = m[:, :GM_W]
    vln = _group_ln(m[:, GM_W:], gng_ref[...], gnb_ref[...], gmat).astype(BF16)
    wr = _iota_mod((GM_HEADS * CHUNK, CHUNK), 0, CHUNK)
    wc = lax.broadcasted_iota(jnp.int32, (GM_HEADS * CHUNK, CHUNK), 1)
    w_tril = jnp.where(wc <= wr, ws_ref[...], 0.0).astype(BF16)
    lane_head = _iota_div((CHUNK, GM_W), 1, GM_DH)
    outs = []
    for r in range(0, ts, CHUNK):
        sv = jnp.dot(w_tril, vln[r:r + CHUNK, :], preferred_element_type=F32)
        s = sv[0:CHUNK, :]
        for h in range(1, GM_HEADS):
            s = jnp.where(lane_head == h, sv[h * CHUNK:(h + 1) * CHUNK, :], s)
        outs.append(u[r:r + CHUNK, :] * (s + bs_ref[...]))
    gm_out[0] = jnp.concatenate(outs, axis=0).astype(BF16)


def _const_spec(shape):
    nd = len(shape)
    return pl.BlockSpec(shape, lambda *_: (0,) * nd, pipeline_mode=pl.Buffered(1))


def _prompt_mix_kernel_with_kv(*refs, n_in, rows_per_conv_chunk):
    _prompt_mix_kernel(*refs[:n_in], *refs[n_in + 2:], rows_per_conv_chunk=rows_per_conv_chunk)


def _prompt_mix(x, ng, w_in, qg, kg, cw, cb, cng, cnb, gng, gnb, ws, bs, *, ts, layer, depth, kv_all=None):
    B, S, D = x.shape
    grid = (B, S // ts)
    row = lambda w: pl.BlockSpec((1, ts, w), lambda b, j: (b, j, 0))
    if kv_all is None:
        by_head = pl.BlockSpec((depth, None, ts * ATT_HEADS, ATT_DV), lambda b, j: (0, b, j, 0))
    else:
        by_head = pl.BlockSpec((1, None, ts * ATT_HEADS, ATT_DV), lambda b, j: (layer, b, j, 0))
    inputs = (x, ng, w_in, qg, kg, cw, cb, cng, cnb, gng, gnb, ws, bs)
    in_specs = [row(D)] + [_const_spec(a.shape) for a in inputs[1:]]
    body = partial(_prompt_mix_kernel, rows_per_conv_chunk=64)
    aliases = {}
    if kv_all is not None:
        body = partial(_prompt_mix_kernel_with_kv, n_in=len(inputs), rows_per_conv_chunk=64)
        aliases = {len(inputs): 1, len(inputs) + 1: 3}
        in_specs = in_specs + [pl.BlockSpec(memory_space=pl.ANY)] * 2
        inputs = inputs + tuple(kv_all)
    outs = (
        jax.ShapeDtypeStruct((B, S, QK_W), BF16),
        jax.ShapeDtypeStruct((depth, B, S * ATT_HEADS, 2 * ATT_DH), F32),
        jax.ShapeDtypeStruct((B, S, QK_W), BF16),
        jax.ShapeDtypeStruct((depth, B, S * ATT_HEADS, ATT_DV), F32),
        jax.ShapeDtypeStruct((B, S, ATT_W), BF16),
        jax.ShapeDtypeStruct((B, S, CONV_W), BF16),
        jax.ShapeDtypeStruct((B, S, GM_W), BF16),
        jax.ShapeDtypeStruct((B, CONV_K - 1, CONV_W), F32),
    )
    return pl.pallas_call(
        body,
        grid=grid,
        in_specs=in_specs,
        out_specs=(row(QK_W), by_head, row(QK_W), by_head, row(ATT_W), row(CONV_W), row(GM_W),
                   pl.BlockSpec((1, CONV_K - 1, CONV_W), lambda b, j: (b, 0, 0))),
        out_shape=outs,
        input_output_aliases=aliases,
        scratch_shapes=[pltpu.VMEM((HALO + ts, CONV_W), F32),
                        pltpu.VMEM((SUBLANES - 1, HALO + ts - SUBLANES, CONV_W), F32)],
        compiler_params=pltpu.CompilerParams(
            dimension_semantics=("arbitrary", "arbitrary"), vmem_limit_bytes=V7X_VMEM_LIMIT),
        name="prompt_mix",
    )(*inputs)


def _sample_mix_kernel(x_ref, st_ref, ng_ref, w_ref, qg_ref, kg_ref, cw_ref, cb_ref, cng_ref, cnb_ref,
                       gng_ref, gnb_ref, wl_ref, bl_ref,
                       q_out, k_out, v_out, conv_out, gm_out, cst_out, vrow_out, buf, *, n_tok, n_batch):
    rows = n_tok * n_batch
    hist = (CONV_K - 1) * n_batch
    gmat = _group_matrix()
    xn = _rms_rows(x_ref[...], ng_ref[...]).astype(BF16)

    q_out[...] = _group_rms(_project(xn, w_ref, _Q0, _K0), qg_ref[...], gmat) * Q_SCALE
    k_out[...] = _group_rms(_project(xn, w_ref, _K0, _V0), kg_ref[...], gmat)
    v_out[...] = _project(xn, w_ref, _V0, _C0)

    c = _project(xn, w_ref, _C0, _M0)
    buf[0:hist, :] = st_ref[...]
    buf[hist:hist + rows, :] = c[:, :CONV_W] * _sigmoid(c[:, CONV_W:])
    acc = jnp.broadcast_to(cb_ref[...], (rows, CONV_W))
    for t in range(CONV_K):
        acc = acc + cw_ref[t:t + 1, :] * buf[t * n_batch:t * n_batch + rows, :]
    conv_out[...] = _silu(_group_ln(acc, cng_ref[...], cnb_ref[...], gmat)).astype(BF16)
    cst_out[...] = buf[rows:rows + hist, :]

    m = _gelu_tanh(_project(xn, w_ref, _M0, _END))
    u = m[:, :GM_W]
    vln = _group_ln(m[:, GM_W:], gng_ref[...], gnb_ref[...], gmat)
    vrow_out[...] = vln
    outs = []
    for i in range(n_tok):
        s = jnp.broadcast_to(bl_ref[i:i + 1, :], (n_batch, GM_W))
        for jj in range(i + 1):
            s = s + wl_ref[i * n_tok + jj:i * n_tok + jj + 1, :] * vln[jj * n_batch:(jj + 1) * n_batch, :]
        outs.append(u[i * n_batch:(i + 1) * n_batch, :] * s)
    gm_out[...] = jnp.concatenate(outs, axis=0).astype(BF16)


def _sample_mix(x, st, ng, w_in, qg, kg, cw, cb, cng, cnb, gng, gnb, wl, bl, *, n_tok, n_batch):
    rows = n_tok * n_batch
    hist = (CONV_K - 1) * n_batch
    outs = (
        jax.ShapeDtypeStruct((rows, QK_W), F32),
        jax.ShapeDtypeStruct((rows, QK_W), F32),
        jax.ShapeDtypeStruct((rows, ATT_W), F32),
        jax.ShapeDtypeStruct((rows, CONV_W), BF16),
        jax.ShapeDtypeStruct((rows, GM_W), BF16),
        jax.ShapeDtypeStruct((hist, CONV_W), F32),
        jax.ShapeDtypeStruct((rows, GM_W), F32),
    )
    return pl.pallas_call(
        partial(_sample_mix_kernel, n_tok=n_tok, n_batch=n_batch),
        out_shape=outs,
        scratch_shapes=[pltpu.VMEM((hist + rows, CONV_W), F32)],
        compiler_params=pltpu.CompilerParams(vmem_limit_bytes=V7X_VMEM_LIMIT),
        name="sample_mix",
    )(x, st, ng, w_in, qg, kg, cw, cb, cng, cnb, gng, gnb, wl, bl)


def _lane_tiles(x):
    return [x[:, t:t + LANES] for t in range(0, x.shape[-1], LANES)]


def _prompt_attn_kernel(q_ref, k_ref, v_ref, lq1_ref, lk1_ref, lq2_ref, lk2_ref, sg_ref, o_ref, o_scr, *,
                        lam_init, chunk):
    i = pl.program_id(2)
    tq = q_ref.shape[1]
    q2 = _split_maps(q_ref[0])
    lam = _diff_lambda(lq1_ref[...], lk1_ref[...], lq2_ref[...], lk2_ref[...], lam_init)

    def scores(r0, n_keys):
        s = _nt_dot(q2[r0:r0 + chunk], k_ref[0, 0:n_keys, :])
        tiles = _lane_tiles(s)
        n_diag = chunk // LANES
        for t in range(n_diag):
            rpos = lax.broadcasted_iota(jnp.int32, (chunk, LANES), 0)
            cpos = lax.broadcasted_iota(jnp.int32, (chunk, LANES), 1) + t * LANES
            tiles[t - n_diag] = jnp.where(cpos <= rpos, tiles[t - n_diag], -jnp.inf)
        return tiles

    def softmax_pv(r0, n_keys, tiles):
        m_tile = tiles[0]
        for t in tiles[1:]:
            m_tile = jnp.maximum(m_tile, t)
        m = jnp.broadcast_to(jnp.max(m_tile, axis=-1, keepdims=True), (chunk, LANES))
        ps = [jnp.exp2(t - m) for t in tiles]
        l_tile = ps[0]
        for p in ps[1:]:
            l_tile = l_tile + p
        acc = jnp.dot(jnp.concatenate(ps, axis=-1).astype(BF16), v_ref[0, 0:n_keys, :],
                      preferred_element_type=F32)
        o_scr[r0:r0 + chunk, :] = acc / jnp.sum(l_tile, axis=-1, keepdims=True)

    def attend(blk):
        work = [(r0, ((blk * tq + r0 % tq) // chunk + 1) * chunk) for r0 in range(0, 2 * tq, chunk)]
        work = sorted(work, key=lambda w: (-w[1], w[0]))
        ahead = 2
        pending = [scores(*w) for w in work[:ahead]]
        for idx, (r0, n_keys) in enumerate(work):
            if idx + ahead < len(work):
                pending.append(scores(*work[idx + ahead]))
            softmax_pv(r0, n_keys, pending.pop(0))
        w = o_scr[0:tq, :] - lam * o_scr[tq:2 * tq, :]
        o_ref[0] = (_rms_rows(w, sg_ref[...]) * (1.0 - lam_init)).astype(BF16)

    for blk in range(k_ref.shape[1] // tq):
        pl.when(i == blk)(partial(attend, blk))


def _prompt_attn(q, kb, vb, lq1, lk1, lq2, lk2, sg, *, lam_init, tq):
    B, S, _ = q.shape
    grid = (B, ATT_HEADS, S // tq)
    vec = lambda a: pl.BlockSpec(a.shape, lambda b, h, i: (0, 0))
    return pl.pallas_call(
        partial(_prompt_attn_kernel, lam_init=lam_init, chunk=256),
        grid=grid,
        in_specs=[pl.BlockSpec((1, tq, ATT_DV), lambda b, h, i: (b, i, h)),
                  pl.BlockSpec((1, S, ATT_DV), lambda b, h, i: (b, 0, h)),
                  pl.BlockSpec((1, S, ATT_DV), lambda b, h, i: (b, 0, h)),
                  vec(lq1), vec(lk1), vec(lq2), vec(lk2), vec(sg)],
        out_specs=pl.BlockSpec((1, tq, ATT_DV), lambda b, h, i: (b, i, h)),
        out_shape=jax.ShapeDtypeStruct((B, S, ATT_W), BF16),
        scratch_shapes=[pltpu.VMEM((2 * tq, ATT_DV), F32)],
        compiler_params=pltpu.CompilerParams(
            dimension_semantics=("arbitrary", "arbitrary", "arbitrary"), vmem_limit_bytes=V7X_VMEM_LIMIT),
        name="prompt_attn",
    )(q, kb, vb, lq1, lk1, lq2, lk2, sg)


def _self_attention_init(qx, kn_ref, vn_ref, m_scr, l_scr, acc_scr, n_tok):
    rows = qx.shape[0]
    qf = qx.astype(F32)
    tok = _iota_mod((rows, 1), 0, rows // 2) >> (ATT_HEADS.bit_length() - 1)
    ss = []
    for j in range(n_tok):
        sj = jnp.sum(qf * kn_ref[0, j], axis=-1, keepdims=True)
        ss.append(jnp.where(tok >= j, sj, -jnp.inf))
    m0 = ss[0]
    for sj in ss[1:]:
        m0 = jnp.maximum(m0, sj)
    l0 = jnp.zeros((rows, 1), F32)
    a0 = jnp.zeros((rows, ATT_DV), F32)
    for j in range(n_tok):
        pj = jnp.exp2(ss[j] - m0)
        l0 = l0 + pj
        a0 = a0 + pj * vn_ref[0, j]
    m_scr[...] = jnp.broadcast_to(m0, (rows, LANES))
    l_scr[...] = jnp.where(lax.broadcasted_iota(jnp.int32, (rows, LANES), 1) == 0, l0, 0.0)
    acc_scr[...] = a0


def _page_scores(qx, k_pages):
    rows = qx.shape[0]
    own = _iota_mod((rows, LANES), 1, ATT_HEADS) == _iota_mod((rows, LANES), 0, ATT_HEADS)
    page_tiles = []
    for kp in k_pages:
        s = _nt_dot(qx, kp.astype(BF16))
        page_tiles.append([jnp.where(own, t, -jnp.inf) for t in _lane_tiles(s)])
    return page_tiles


def _attend_pages(page_tiles, v_pages, m_scr, l_scr, acc_scr):
    rows = acc_scr.shape[0]
    m_tile = None
    for tiles in page_tiles:
        for t in tiles:
            m_tile = t if m_tile is None else jnp.maximum(m_tile, t)
    m_prev = m_scr[...]
    m_new = jnp.maximum(m_prev, jnp.broadcast_to(jnp.max(m_tile, axis=-1, keepdims=True), (rows, LANES)))
    alpha = jnp.exp2(m_prev - m_new)
    l_tile = alpha * l_scr[...]
    acc = alpha * acc_scr[...]
    for tiles, vp in zip(page_tiles, v_pages):
        ps = [jnp.exp2(t - m_new) for t in tiles]
        for p in ps:
            l_tile = l_tile + p
        acc = acc + jnp.dot(jnp.concatenate(ps, axis=-1).astype(BF16), vp.astype(BF16),
                            preferred_element_type=F32)
    m_scr[...] = m_new
    l_scr[...] = l_tile
    acc_scr[...] = acc


def _sample_attention_output(lam, sg, l_scr, acc_scr, lam_init):
    half = acc_scr.shape[0] // 2
    o = acc_scr[...] / jnp.sum(l_scr[...], axis=-1, keepdims=True)
    w = o[:half] - lam * o[half:]
    return _rms_rows(w, sg) * (1.0 - lam_init)


def _ffn_tail(x_ref, att_ref, conv_ref, gm_ref, wo_ref, fg_ref, wg_ref, wu_ref, wd_ref, y_ref, act_scr, side_work=()):
    mix = (jnp.dot(att_ref[...], wo_ref[0:ATT_W, :], preferred_element_type=F32)
           + jnp.dot(conv_ref[...], wo_ref[ATT_W:ATT_W + CONV_W, :], preferred_element_type=F32)
           + jnp.dot(gm_ref[...], wo_ref[ATT_W + CONV_W:, :], preferred_element_type=F32))
    x1 = x_ref[...] + mix
    h = _rms_rows(x1, fg_ref[...]).astype(BF16)
    chunk_starts = range(0, wg_ref.shape[1], FF_CHUNK)
    assert len(side_work) <= len(chunk_starts)
    for n, c in enumerate(chunk_starts):
        gate = jnp.dot(h, wg_ref[:, c:c + FF_CHUNK], preferred_element_type=F32)
        up = jnp.dot(h, wu_ref[:, c:c + FF_CHUNK], preferred_element_type=F32)
        act_scr[:, c:c + FF_CHUNK] = (_silu(gate) * up).astype(BF16)
        if n < len(side_work):
            side_work[n]()
    y_ref[...] = x1 + jnp.dot(act_scr[...], wd_ref[...], preferred_element_type=F32)


def _tail_kernel(x_ref, att_ref, conv_ref, gm_ref, wo_ref, fg_ref, wg_ref, wu_ref, wd_ref, y_ref, act_scr):
    _ffn_tail(x_ref, att_ref, conv_ref, gm_ref, wo_ref, fg_ref, wg_ref, wu_ref, wd_ref, y_ref, act_scr)


def _tail(x, att, conv, gm, wo, fg, wg, wu, wd, *, tm):
    n, d = x.shape
    d_ff = wg.shape[1]
    assert d_ff % FF_CHUNK == 0
    row = lambda w: pl.BlockSpec((tm, w), lambda i: (i, 0))
    return pl.pallas_call(
        _tail_kernel,
        grid=(n // tm,),
        in_specs=[row(d), row(ATT_W), row(CONV_W), row(GM_W)]
                 + [_const_spec(a.shape) for a in (wo, fg, wg, wu, wd)],
        out_specs=row(d),
        out_shape=jax.ShapeDtypeStruct((n, d), F32),
        scratch_shapes=[pltpu.VMEM((tm, d_ff), BF16)],
        compiler_params=pltpu.CompilerParams(
            dimension_semantics=("arbitrary",), vmem_limit_bytes=V7X_VMEM_LIMIT),
        name="tail",
    )(x, att, conv, gm, wo, fg, wg, wu, wd)


PAGE_GROUP = 8
PAGE_SLOTS = 4
PAGE_LIVE = 2
PAGE_AHEAD = PAGE_SLOTS - PAGE_LIVE


def _tail_attn_kernel(pt_ref, x_ref, att_ref, conv_ref, gm_ref, wo_ref, fg_ref, wg_ref, wu_ref, wd_ref,
                      qx_ref, kn_ref, vn_ref, lq1_ref, lk1_ref, lq2_ref, lk2_ref, sg_ref, ck_hbm, cv_hbm,
                      y_ref, o_ref,
                      act_scr, kbuf, vbuf, sems, m_scr, l_scr, acc_scr, *,
                      layer, lam_init, n_tok, groups_per_step, steps_per_batch):
    s = pl.program_id(0)
    n_slots, group = kbuf.shape[0], kbuf.shape[1]
    ahead = PAGE_AHEAD

    def slot_of(g):
        return lax.rem(s * groups_per_step + g, n_slots)

    def page_copies(g):
        gi = s * groups_per_step + g
        slot = slot_of(g)
        copies = []
        for i in range(group):
            page = pt_ref[gi * group + i]
            copies.append(pltpu.make_async_copy(ck_hbm.at[layer, page], kbuf.at[slot, i], sems.at[0, slot]))
            copies.append(pltpu.make_async_copy(cv_hbm.at[layer, page], vbuf.at[slot, i], sems.at[1, slot]))
        return copies

    @pl.when(s == 0)
    def _():
        for g in range(ahead):
            for cp in page_copies(g):
                cp.start()

    qx = qx_ref[0]

    @pl.when(lax.rem(s, steps_per_batch) == 0)
    def _():
        _self_attention_init(qx, kn_ref, vn_ref, m_scr, l_scr, acc_scr, n_tok)

    def group_scores(g):
        for cp in page_copies(g):
            cp.wait()
        for cp in page_copies(g + ahead):
            cp.start()
        slot = slot_of(g)
        return _page_scores(qx, [kbuf[slot, i] for i in range(group)])

    pending = [group_scores(0)]

    def attend_group(g):
        if g + 1 < groups_per_step:
            pending.append(group_scores(g + 1))
        slot = slot_of(g)
        _attend_pages(pending.pop(0), [vbuf[slot, i] for i in range(group)], m_scr, l_scr, acc_scr)

    _ffn_tail(x_ref, att_ref, conv_ref, gm_ref, wo_ref, fg_ref, wg_ref, wu_ref, wd_ref, y_ref, act_scr,
              side_work=[partial(attend_group, g) for g in range(groups_per_step)])

    @pl.when(lax.rem(s, steps_per_batch) == steps_per_batch - 1)
    def _():
        lam = _diff_lambda(lq1_ref[...], lk1_ref[...], lq2_ref[...], lk2_ref[...], lam_init)
        o_ref[0] = _sample_attention_output(lam, sg_ref[...], l_scr, acc_scr, lam_init)

    @pl.when(s == pl.num_programs(0) - 1)
    def _():
        for g in range(groups_per_step, groups_per_step + ahead):
            for cp in page_copies(g):
                cp.wait()


def _tail_attn(x, att, conv, gm, wo, fg, wg, wu, wd, page_table, qx, kn, vn, lq1, lk1, lq2, lk2, sg,
               cache_k, cache_v, *, tm, layer, lam_init, n_tok):
    n, d = x.shape
    d_ff = wg.shape[1]
    Bd, rows, _ = qx.shape
    n_pages = page_table.shape[1]
    page_rows = cache_k.shape[2]
    n_steps = n // tm
    assert d_ff % FF_CHUNK == 0 and n_steps % Bd == 0
    steps_per_batch = n_steps // Bd
    assert n_pages % (steps_per_batch * PAGE_GROUP) == 0
    groups_per_step = n_pages // (steps_per_batch * PAGE_GROUP)
    assert groups_per_step <= d_ff // FF_CHUNK
    pt_flat = jnp.concatenate([page_table.reshape(-1),
                               jnp.zeros((PAGE_AHEAD * PAGE_GROUP,), page_table.dtype)])
    row = lambda w: pl.BlockSpec((tm, w), lambda i, pt: (i, 0))
    vec = lambda a: pl.BlockSpec(a.shape, lambda i, pt: (0, 0))
    per_b = lambda a: pl.BlockSpec((1,) + a.shape[1:], lambda i, pt: (i // steps_per_batch,) + (0,) * (a.ndim - 1))
    hbm = pl.BlockSpec(memory_space=pl.ANY)
    return pl.pallas_call(
        partial(_tail_attn_kernel, layer=layer, lam_init=lam_init, n_tok=n_tok,
                groups_per_step=groups_per_step, steps_per_batch=steps_per_batch),
        grid_spec=pltpu.PrefetchScalarGridSpec(
            num_scalar_prefetch=1, grid=(n_steps,),
            in_specs=[row(d), row(ATT_W), row(CONV_W), row(GM_W)]
                     + [_const_spec(a.shape) for a in (wo, fg, wg, wu, wd)]
                     + [per_b(qx), per_b(kn), per_b(vn), vec(lq1), vec(lk1), vec(lq2), vec(lk2), vec(sg), hbm, hbm],
            out_specs=(row(d), pl.BlockSpec((1, rows // 2, ATT_DV), lambda i, pt: (i // steps_per_batch, 0, 0))),
            scratch_shapes=[pltpu.VMEM((tm, d_ff), BF16),
                            pltpu.VMEM((PAGE_SLOTS, PAGE_GROUP, page_rows, ATT_DV), F32),
                            pltpu.VMEM((PAGE_SLOTS, PAGE_GROUP, page_rows, ATT_DV), F32),
                            pltpu.SemaphoreType.DMA((2, PAGE_SLOTS)),
                            pltpu.VMEM((rows, LANES), F32), pltpu.VMEM((rows, LANES), F32),
                            pltpu.VMEM((rows, ATT_DV), F32)]),
        out_shape=(jax.ShapeDtypeStruct((n, d), F32), jax.ShapeDtypeStruct((Bd, rows // 2, ATT_DV), F32)),
        compiler_params=pltpu.CompilerParams(
            dimension_semantics=("arbitrary",), vmem_limit_bytes=V7X_VMEM_LIMIT),
        name="tail_attn",
    )(pt_flat, x, att, conv, gm, wo, fg, wg, wu, wd, qx, kn, vn, lq1, lk1, lq2, lk2, sg, cache_k, cache_v)


def kernel(x_prompt, x_sample, cache_k, cache_v, state_conv, page_table, mix_norm_g, w_in, q_norm_g, k_norm_g,
           lam_q1, lam_k1, lam_q2, lam_k2, subln_g, conv_w, conv_b, conv_norm_g, conv_norm_b, gm_norm_g,
           gm_norm_b, gm_ws, gm_bs, w_out, ffn_norm_g, w_gate, w_up, w_down):
    B, S, D = x_prompt.shape
    Bd, T, _ = x_sample.shape
    depth = w_in.shape[0]
    n_phys, page = cache_k.shape[1], cache_k.shape[2]
    assert cache_k.shape[3:] == (ATT_HEADS, 2 * ATT_DH) and cache_v.shape[3:] == (ATT_HEADS, ATT_DV)
    assert w_in.shape[-1] == _END and conv_w.shape[1:] == (CONV_K, CONV_W)
    assert gm_ws.shape[1:] == (GM_HEADS, CHUNK, CHUNK) and S % CHUNK == 0 and T <= CHUNK

    ck = cache_k.reshape(depth, n_phys, page * ATT_HEADS, 2 * ATT_DH)
    cv = cache_v.reshape(depth, n_phys, page * ATT_HEADS, ATT_DV)
    row = lambda a: a.reshape(1, -1)

    xp = x_prompt
    xs = jnp.swapaxes(x_sample, 0, 1).reshape(T * Bd, D)

    kv_all = None
    c_p, k_s, v_s, c_s, g_s = [], [], [], [], []
    for l in range(depth):
        lam_init = 0.8 - 0.6 * math.exp(-0.3 * l)
        w_in_b = w_in[l].astype(BF16)
        qg = row(jnp.tile(q_norm_g[l], QK_W // ATT_DH))
        kg = row(jnp.tile(k_norm_g[l], QK_W // ATT_DH))
        lam_vecs = (row(lam_q1[l]), row(lam_k1[l]), row(lam_q2[l]), row(lam_k2[l]))
        sg = row(subln_g[l])
        mix_args = (row(mix_norm_g[l]), w_in_b, qg, kg, conv_w[l], row(conv_b[l]), row(conv_norm_g[l]),
                    row(conv_norm_b[l]), row(gm_norm_g[l]), row(gm_norm_b[l]))
        tail_w = (w_out[l].astype(BF16), row(ffn_norm_g[l]), w_gate[l].astype(BF16), w_up[l].astype(BF16),
                  w_down[l].astype(BF16))

        ws_stack = gm_ws[l].reshape(GM_HEADS * CHUNK, CHUNK)
        bs_rows = jnp.repeat(gm_bs[l].T, GM_DH, axis=1)
        q, k_all, kb, v_all, vb, conv_p, gm_p, cst_p = _prompt_mix(
            xp, *mix_args, ws_stack, bs_rows, ts=1024, layer=l, depth=depth, kv_all=kv_all)
        kv_all = (k_all, v_all)
        att_p = _prompt_attn(q, kb, vb, *lam_vecs, sg, lam_init=lam_init, tq=S)

        st = jnp.swapaxes(state_conv[l], 0, 1).reshape((CONV_K - 1) * Bd, CONV_W)
        wl = jnp.repeat(jnp.transpose(gm_ws[l][:, :T, :T], (1, 2, 0)).reshape(T * T, GM_HEADS), GM_DH, axis=1)
        bl = jnp.repeat(gm_bs[l][:, :T].T, GM_DH, axis=1)
        qs, ks, vs, conv_s, gm_s, cst, vrows = _sample_mix(xs, st, *mix_args, wl, bl, n_tok=T, n_batch=Bd)
        tb = lambda a: jnp.swapaxes(a.reshape(T, Bd, -1), 0, 1)
        k_bt = tb(ks).reshape(Bd, T, ATT_HEADS, 2 * ATT_DH)
        v_bt = tb(vs).reshape(Bd, T, ATT_HEADS, ATT_DV)
        qx = _split_maps(tb(qs).reshape(Bd, T * ATT_HEADS, 2 * ATT_DH)).astype(BF16)
        reps = 2 * T

        xp, att_s = _tail_attn(
            xp.reshape(B * S, D), att_p.reshape(B * S, ATT_W), conv_p.reshape(B * S, CONV_W),
            gm_p.reshape(B * S, GM_W), *tail_w, page_table, qx, jnp.tile(k_bt, (1, 1, reps, 1)),
            jnp.tile(v_bt, (1, 1, reps, 1)), *lam_vecs, sg, ck, cv, tm=512, layer=l, lam_init=lam_init, n_tok=T)
        xp = xp.reshape(B, S, D)
        att_s = jnp.swapaxes(att_s.reshape(Bd, T, ATT_W), 0, 1).reshape(T * Bd, ATT_W).astype(BF16)
        xs = _tail(xs, att_s, conv_s, gm_s, *tail_w, tm=T * Bd)
        c_p.append(cst_p)
        k_s.append(k_bt)
        v_s.append(v_bt)
        c_s.append(jnp.swapaxes(cst.reshape(CONV_K - 1, Bd, CONV_W), 0, 1))
        g_s.append(tb(vrows))

    ys = jnp.swapaxes(xs.reshape(T, Bd, D), 0, 1)
    k_all, v_all = kv_all
    return (xp, ys, k_all.reshape(depth, B, S, ATT_HEADS, 2 * ATT_DH), v_all.reshape(depth, B, S, ATT_HEADS, ATT_DV),
            jnp.stack(c_p),
            jnp.stack(k_s), jnp.stack(v_s), jnp.stack(c_s), jnp.stack(g_s))
```

```python
import math
from functools import partial

import jax
import jax.numpy as jnp
from jax import lax
from jax.experimental import pallas as pl
from jax.experimental.pallas import tpu as pltpu

F32 = jnp.float32
BF16 = jnp.bfloat16

EPS = 1e-6
ATT_HEADS = 4
ATT_DH = 64
ATT_DV = 2 * ATT_DH
ATT_W = ATT_HEADS * ATT_DV
QK_W = ATT_HEADS * 2 * ATT_DH
CONV_W = 256
CONV_K = 31
GM_HEADS = 4
GM_DH = 64
GM_W = GM_HEADS * GM_DH
CHUNK = 128
GROUP = 64
GROUP_TILE = 256
HALO = 32
LANES = 128
SUBLANES = 8
FF_CHUNK = 256

_Q0, _K0, _V0, _C0, _M0, _END = 0, QK_W, 2 * QK_W, 2 * QK_W + ATT_W, 2 * QK_W + ATT_W + 2 * CONV_W, \
    2 * QK_W + ATT_W + 2 * CONV_W + 2 * GM_W

Q_SCALE = ATT_DH ** -0.5 * math.log2(math.e)

V7X_VMEM_LIMIT = 60 * 1024 * 1024


def _iota_div(shape, axis, div):
    assert div & (div - 1) == 0
    return lax.broadcasted_iota(jnp.int32, shape, axis) >> (div.bit_length() - 1)


def _iota_mod(shape, axis, mod):
    assert mod & (mod - 1) == 0
    return lax.broadcasted_iota(jnp.int32, shape, axis) & (mod - 1)


def _group_matrix():
    r = _iota_div((GROUP_TILE, GROUP_TILE), 0, GROUP)
    c = _iota_div((GROUP_TILE, GROUP_TILE), 1, GROUP)
    return jnp.where(r == c, 1.0, 0.0).astype(BF16)


def _group_sum(x, gmat):
    return jnp.dot(x.astype(BF16), gmat, preferred_element_type=F32)


def _group_sum_split(x, gmat):
    hi = x.astype(BF16)
    lo = (x - hi.astype(F32)).astype(BF16)
    return (jnp.dot(hi, gmat, preferred_element_type=F32)
            + jnp.dot(lo, gmat, preferred_element_type=F32))


def _rms_rows(x, g):
    return x * lax.rsqrt(jnp.mean(x * x, axis=-1, keepdims=True) + EPS) * g


def _group_rms(z, g, gmat):
    parts = []
    for c in range(0, z.shape[-1], GROUP_TILE):
        zc = z[:, c:c + GROUP_TILE]
        ms = _group_sum(zc * zc, gmat) * (1.0 / GROUP)
        parts.append(zc * lax.rsqrt(ms + EPS))
    y = parts[0] if len(parts) == 1 else jnp.concatenate(parts, axis=-1)
    return y * g


def _group_ln(z, g, b, gmat):
    parts = []
    for c in range(0, z.shape[-1], GROUP_TILE):
        zc = z[:, c:c + GROUP_TILE]
        d = zc - _group_sum_split(zc, gmat) * (1.0 / GROUP)
        var = _group_sum(d * d, gmat) * (1.0 / GROUP)
        parts.append(d * lax.rsqrt(var + EPS))
    y = parts[0] if len(parts) == 1 else jnp.concatenate(parts, axis=-1)
    return y * g + b


def _sigmoid(x):
    return 1.0 / (1.0 + jnp.exp(-x))


def _silu(x):
    return x * _sigmoid(x)


def _gelu_tanh(x):
    return 0.5 * x * (1.0 + jnp.tanh(math.sqrt(2.0 / math.pi) * (x + 0.044715 * (x * x * x))))


def _project(xn, w_ref, lo, hi):
    return jnp.dot(xn, w_ref[:, lo:hi], preferred_element_type=F32)


def _diff_lambda(lq1, lk1, lq2, lk2, lam_init):
    a = jnp.sum(lq1 * lk1, axis=-1, keepdims=True)
    b = jnp.sum(lq2 * lk2, axis=-1, keepdims=True)
    return jnp.exp(a) - jnp.exp(b) + lam_init


def _split_maps(q):
    lane = lax.broadcasted_iota(jnp.int32, q.shape, q.ndim - 1)
    zero = jnp.zeros_like(q)
    return jnp.concatenate([jnp.where(lane < ATT_DH, q, zero), jnp.where(lane >= ATT_DH, q, zero)],
                           axis=q.ndim - 2)


def _nt_dot(a, b):
    return lax.dot_general(a, b, (((1,), (1,)), ((), ())), preferred_element_type=F32)


def _prompt_mix_kernel(x_ref, ng_ref, w_ref, qg_ref, kg_ref, cw_ref, cb_ref, cng_ref, cnb_ref,
                       gng_ref, gnb_ref, ws_ref, bs_ref,
                       q_out, k_out, kb_out, v_out, vb_out, conv_out, gm_out, cst_out,
                       glu_buf, shifted, *, rows_per_conv_chunk):
    j = pl.program_id(1)
    nj = pl.num_programs(1)
    ts = x_ref.shape[1]
    gmat = _group_matrix()

    @pl.when(j == 0)
    def _():
        glu_buf[0:HALO, :] = jnp.zeros((HALO, CONV_W), F32)

    xn = _rms_rows(x_ref[0], ng_ref[...]).astype(BF16)

    c = _project(xn, w_ref, _C0, _M0)
    glu_buf[HALO:HALO + ts, :] = c[:, :CONV_W] * _sigmoid(c[:, CONV_W:])
    first = HALO - (CONV_K - 1)
    n_shift = shifted.shape[1]
    for s in range(1, SUBLANES):
        shifted[s - 1] = glu_buf[s:s + n_shift, :]
    rc = rows_per_conv_chunk

    def conv_rows(r):
        acc = jnp.broadcast_to(cb_ref[...], (rc, CONV_W))
        for t in range(CONV_K):
            base, phase = (first + t) // SUBLANES * SUBLANES, (first + t) % SUBLANES
            src = glu_buf if phase == 0 else shifted.at[phase - 1]
            acc = acc + cw_ref[t:t + 1, :] * src[r + base:r + base + rc, :]
        return acc

    row_starts = list(range(0, ts, rc))
    third = -(-len(row_starts) // 3)
    ys = [conv_rows(r) for r in row_starts[:third]]

    q = _group_rms(_project(xn, w_ref, _Q0, _K0), qg_ref[...], gmat) * Q_SCALE
    q_out[0] = q.astype(BF16)
    ys += [conv_rows(r) for r in row_starts[third:2 * third]]

    k = _group_rms(_project(xn, w_ref, _K0, _V0), kg_ref[...], gmat)
    kb_out[0] = k.astype(BF16)
    ys += [conv_rows(r) for r in row_starts[2 * third:]]

    v = _project(xn, w_ref, _V0, _C0)
    vb_out[0] = v.astype(BF16)
    for slab in range(k_out.shape[0]):
        for h in range(ATT_HEADS):
            k_out[slab, pl.ds(h, ts, stride=ATT_HEADS), :] = k[:, h * ATT_DV:(h + 1) * ATT_DV]
            v_out[slab, pl.ds(h, ts, stride=ATT_HEADS), :] = v[:, h * ATT_DV:(h + 1) * ATT_DV]

    y = jnp.concatenate(ys, axis=0)
    conv_out[0] = _silu(_group_ln(y, cng_ref[...], cnb_ref[...], gmat)).astype(BF16)
    cst_out[0] = glu_buf[ts + first:ts + HALO, :]
    glu_buf[0:HALO, :] = glu_buf[ts:ts + HALO, :]

    m = _gelu_tanh(_project(xn, w_ref, _M0, _END))
    u = m[:, :GM_W]
    vln = _group_ln(m[:, GM_W:], gng_ref[...], gnb_ref[...], gmat).astype(BF16)
    wr = _iota_mod((GM_HEADS * CHUNK, CHUNK), 0, CHUNK)
    wc = lax.broadcasted_iota(jnp.int32, (GM_HEADS * CHUNK, CHUNK), 1)
    w_tril = jnp.where(wc <= wr, ws_ref[...], 0.0).astype(BF16)
    lane_head = _iota_div((CHUNK, GM_W), 1, GM_DH)
    outs = []
    for r in range(0, ts, CHUNK):
        sv = jnp.dot(w_tril, vln[r:r + CHUNK, :], preferred_element_type=F32)
        s = sv[0:CHUNK, :]
        for h in range(1, GM_HEADS):
            s = jnp.where(lane_head == h, sv[h * CHUNK:(h + 1) * CHUNK, :], s)
        outs.append(u[r:r + CHUNK, :] * (s + bs_ref[...]))
    gm_out[0] = jnp.concatenate(outs, axis=0).astype(BF16)


def _const_spec(shape):
    nd = len(shape)
    return pl.BlockSpec(shape, lambda *_: (0,) * nd, pipeline_mode=pl.Buffered(1))


def _prompt_mix_kernel_with_kv(*refs, n_in, rows_per_conv_chunk):
    _prompt_mix_kernel(*refs[:n_in], *refs[n_in + 2:], rows_per_conv_chunk=rows_per_conv_chunk)


def _prompt_mix(x, ng, w_in, qg, kg, cw, cb, cng, cnb, gng, gnb, ws, bs, *, ts, layer, depth, kv_all=None):
    B, S, D = x.shape
    grid = (B, S // ts)
    row = lambda w: pl.BlockSpec((1, ts, w), lambda b, j: (b, j, 0))
    if kv_all is None:
        by_head = pl.BlockSpec((depth, None, ts * ATT_HEADS, ATT_DV), lambda b, j: (0, b, j, 0))
    else:
        by_head = pl.BlockSpec((1, None, ts * ATT_HEADS, ATT_DV), lambda b, j: (layer, b, j, 0))
    inputs = (x, ng, w_in, qg, kg, cw, cb, cng, cnb, gng, gnb, ws, bs)
    in_specs = [row(D)] + [_const_spec(a.shape) for a in inputs[1:]]
    body = partial(_prompt_mix_kernel, rows_per_conv_chunk=64)
    aliases = {}
    if kv_all is not None:
        body = partial(_prompt_mix_kernel_with_kv, n_in=len(inputs), rows_per_conv_chunk=64)
        aliases = {len(inputs): 1, len(inputs) + 1: 3}
        in_specs = in_specs + [pl.BlockSpec(memory_space=pl.ANY)] * 2
        inputs = inputs + tuple(kv_all)
    outs = (
        jax.ShapeDtypeStruct((B, S, QK_W), BF16),
        jax.ShapeDtypeStruct((depth, B, S * ATT_HEADS, 2 * ATT_DH), F32),
        jax.ShapeDtypeStruct((B, S, QK_W), BF16),
        jax.ShapeDtypeStruct((depth, B, S * ATT_HEADS, ATT_DV), F32),
        jax.ShapeDtypeStruct((B, S, ATT_W), BF16),
        jax.ShapeDtypeStruct((B, S, CONV_W), BF16),
        jax.ShapeDtypeStruct((B, S, GM_W), BF16),
        jax.ShapeDtypeStruct((B, CONV_K - 1, CONV_W), F32),
    )
    return pl.pallas_call(
        body,
        grid=grid,
        in_specs=in_specs,
        out_specs=(row(QK_W), by_head, row(QK_W), by_head, row(ATT_W), row(CONV_W), row(GM_W),
                   pl.BlockSpec((1, CONV_K - 1, CONV_W), lambda b, j: (b, 0, 0))),
        out_shape=outs,
        input_output_aliases=aliases,
        scratch_shapes=[pltpu.VMEM((HALO + ts, CONV_W), F32),
                        pltpu.VMEM((SUBLANES - 1, HALO + ts - SUBLANES, CONV_W), F32)],
        compiler_params=pltpu.CompilerParams(
            dimension_semantics=("arbitrary", "arbitrary"), vmem_limit_bytes=V7X_VMEM_LIMIT),
        name="prompt_mix",
    )(*inputs)


def _sample_mix_kernel(x_ref, st_ref, ng_ref, w_ref, qg_ref, kg_ref, cw_ref, cb_ref, cng_ref, cnb_ref,
                       gng_ref, gnb_ref, wl_ref, bl_ref,
                       q_out, k_out, v_out, conv_out, gm_out, cst_out, vrow_out, buf, *, n_tok, n_batch):
    rows = n_tok * n_batch
    hist = (CONV_K - 1) * n_batch
    gmat = _group_matrix()
    xn = _rms_rows(x_ref[...], ng_ref[...]).astype(BF16)

    q_out[...] = _group_rms(_project(xn, w_ref, _Q0, _K0), qg_ref[...], gmat) * Q_SCALE
    k_out[...] = _group_rms(_project(xn, w_ref, _K0, _V0), kg_ref[...], gmat)
    v_out[...] = _project(xn, w_ref, _V0, _C0)

    c = _project(xn, w_ref, _C0, _M0)
    buf[0:hist, :] = st_ref[...]
    buf[hist:hist + rows, :] = c[:, :CONV_W] * _sigmoid(c[:, CONV_W:])
    acc = jnp.broadcast_to(cb_ref[...], (rows, CONV_W))
    for t in range(CONV_K):
        acc = acc + cw_ref[t:t + 1, :] * buf[t * n_batch:t * n_batch + rows, :]
    conv_out[...] = _silu(_group_ln(acc, cng_ref[...], cnb_ref[...], gmat)).astype(BF16)
    cst_out[...] = buf[rows:rows + hist, :]

    m = _gelu_tanh(_project(xn, w_ref, _M0, _END))
    u = m[:, :GM_W]
    vln = _group_ln(m[:, GM_W:], gng_ref[...], gnb_ref[...], gmat)
    vrow_out[...] = vln
    outs = []
    for i in range(n_tok):
        s = jnp.broadcast_to(bl_ref[i:i + 1, :], (n_batch, GM_W))
        for jj in range(i + 1):
            s = s + wl_ref[i * n_tok + jj:i * n_tok + jj + 1, :] * vln[jj * n_batch:(jj + 1) * n_batch, :]
        outs.append(u[i * n_batch:(i + 1) * n_batch, :] * s)
    gm_out[...] = jnp.concatenate(outs, axis=0).astype(BF16)


def _sample_mix(x, st, ng, w_in, qg, kg, cw, cb, cng, cnb, gng, gnb, wl, bl, *, n_tok, n_batch):
    rows = n_tok * n_batch
    hist = (CONV_K - 1) * n_batch
    outs = (
        jax.ShapeDtypeStruct((rows, QK_W), F32),
        jax.ShapeDtypeStruct((rows, QK_W), F32),
        jax.ShapeDtypeStruct((rows, ATT_W), F32),
        jax.ShapeDtypeStruct((rows, CONV_W), BF16),
        jax.ShapeDtypeStruct((rows, GM_W), BF16),
        jax.ShapeDtypeStruct((hist, CONV_W), F32),
        jax.ShapeDtypeStruct((rows, GM_W), F32),
    )
    return pl.pallas_call(
        partial(_sample_mix_kernel, n_tok=n_tok, n_batch=n_batch),
        out_shape=outs,
        scratch_shapes=[pltpu.VMEM((hist + rows, CONV_W), F32)],
        compiler_params=pltpu.CompilerParams(vmem_limit_bytes=V7X_VMEM_LIMIT),
        name="sample_mix",
    )(x, st, ng, w_in, qg, kg, cw, cb, cng, cnb, gng, gnb, wl, bl)


def _lane_tiles(x):
    return [x[:, t:t + LANES] for t in range(0, x.shape[-1], LANES)]


def _prompt_attn_kernel(q_ref, k_ref, v_ref, lq1_ref, lk1_ref, lq2_ref, lk2_ref, sg_ref, o_ref, o_scr, *,
                        lam_init, chunk):
    i = pl.program_id(2)
    tq = q_ref.shape[1]
    q2 = _split_maps(q_ref[0])
    lam = _diff_lambda(lq1_ref[...], lk1_ref[...], lq2_ref[...], lk2_ref[...], lam_init)

    def scores(r0, n_keys):
        s = _nt_dot(q2[r0:r0 + chunk], k_ref[0, 0:n_keys, :])
        tiles = _lane_tiles(s)
        n_diag = chunk // LANES
        for t in range(n_diag):
            rpos = lax.broadcasted_iota(jnp.int32, (chunk, LANES), 0)
            cpos = lax.broadcasted_iota(jnp.int32, (chunk, LANES), 1) + t * LANES
            tiles[t - n_diag] = jnp.where(cpos <= rpos, tiles[t - n_diag], -jnp.inf)
        return tiles

    def softmax_pv(r0, n_keys, tiles):
        m_tile = tiles[0]
        for t in tiles[1:]:
            m_tile = jnp.maximum(m_tile, t)
        m = jnp.broadcast_to(jnp.max(m_tile, axis=-1, keepdims=True), (chunk, LANES))
        ps = [jnp.exp2(t - m) for t in tiles]
        l_tile = ps[0]
        for p in ps[1:]:
            l_tile = l_tile + p
        acc = jnp.dot(jnp.concatenate(ps, axis=-1).astype(BF16), v_ref[0, 0:n_keys, :],
                      preferred_element_type=F32)
        o_scr[r0:r0 + chunk, :] = acc / jnp.sum(l_tile, axis=-1, keepdims=True)

    def attend(blk):
        work = [(r0, ((blk * tq + r0 % tq) // chunk + 1) * chunk) for r0 in range(0, 2 * tq, chunk)]
        work = sorted(work, key=lambda w: (-w[1], w[0]))
        ahead = 2
        pending = [scores(*w) for w in work[:ahead]]
        for idx, (r0, n_keys) in enumerate(work):
            if idx + ahead < len(work):
                pending.append(scores(*work[idx + ahead]))
            softmax_pv(r0, n_keys, pending.pop(0))
        w = o_scr[0:tq, :] - lam * o_scr[tq:2 * tq, :]
        o_ref[0] = (_rms_rows(w, sg_ref[...]) * (1.0 - lam_init)).astype(BF16)

    for blk in range(k_ref.shape[1] // tq):
        pl.when(i == blk)(partial(attend, blk))


def _prompt_attn(q, kb, vb, lq1, lk1, lq2, lk2, sg, *, lam_init, tq):
    B, S, _ = q.shape
    grid = (B, ATT_HEADS, S // tq)
    vec = lambda a: pl.BlockSpec(a.shape, lambda b, h, i: (0, 0))
    return pl.pallas_call(
        partial(_prompt_attn_kernel, lam_init=lam_init, chunk=256),
        grid=grid,
        in_specs=[pl.BlockSpec((1, tq, ATT_DV), lambda b, h, i: (b, i, h)),
                  pl.BlockSpec((1, S, ATT_DV), lambda b, h, i: (b, 0, h)),
                  pl.BlockSpec((1, S, ATT_DV), lambda b, h, i: (b, 0, h)),
                  vec(lq1), vec(lk1), vec(lq2), vec(lk2), vec(sg)],
        out_specs=pl.BlockSpec((1, tq, ATT_DV), lambda b, h, i: (b, i, h)),
        out_shape=jax.ShapeDtypeStruct((B, S, ATT_W), BF16),
        scratch_shapes=[pltpu.VMEM((2 * tq, ATT_DV), F32)],
        compiler_params=pltpu.CompilerParams(
            dimension_semantics=("arbitrary", "arbitrary", "arbitrary"), vmem_limit_bytes=V7X_VMEM_LIMIT),
        name="prompt_attn",
    )(q, kb, vb, lq1, lk1, lq2, lk2, sg)


def _self_attention_init(qx, kn_ref, vn_ref, m_scr, l_scr, acc_scr, n_tok):
    rows = qx.shape[0]
    qf = qx.astype(F32)
    tok = _iota_mod((rows, 1), 0, rows // 2) >> (ATT_HEADS.bit_length() - 1)
    ss = []
    for j in range(n_tok):
        sj = jnp.sum(qf * kn_ref[0, j], axis=-1, keepdims=True)
        ss.append(jnp.where(tok >= j, sj, -jnp.inf))
    m0 = ss[0]
    for sj in ss[1:]:
        m0 = jnp.maximum(m0, sj)
    l0 = jnp.zeros((rows, 1), F32)
    a0 = jnp.zeros((rows, ATT_DV), F32)
    for j in range(n_tok):
        pj = jnp.exp2(ss[j] - m0)
        l0 = l0 + pj
        a0 = a0 + pj * vn_ref[0, j]
    m_scr[...] = jnp.broadcast_to(m0, (rows, LANES))
    l_scr[...] = jnp.where(lax.broadcasted_iota(jnp.int32, (rows, LANES), 1) == 0, l0, 0.0)
    acc_scr[...] = a0


def _page_scores(qx, k_pages):
    rows = qx.shape[0]
    own = _iota_mod((rows, LANES), 1, ATT_HEADS) == _iota_mod((rows, LANES), 0, ATT_HEADS)
    page_tiles = []
    for kp in k_pages:
        s = _nt_dot(qx, kp.astype(BF16))
        page_tiles.append([jnp.where(own, t, -jnp.inf) for t in _lane_tiles(s)])
    return page_tiles


def _attend_pages(page_tiles, v_pages, m_scr, l_scr, acc_scr):
    rows = acc_scr.shape[0]
    m_tile = None
    for tiles in page_tiles:
        for t in tiles:
            m_tile = t if m_tile is None else jnp.maximum(m_tile, t)
    m_prev = m_scr[...]
    m_new = jnp.maximum(m_prev, jnp.broadcast_to(jnp.max(m_tile, axis=-1, keepdims=True), (rows, LANES)))
    alpha = jnp.exp2(m_prev - m_new)
    l_tile = alpha * l_scr[...]
    acc = alpha * acc_scr[...]
    for tiles, vp in zip(page_tiles, v_pages):
        ps = [jnp.exp2(t - m_new) for t in tiles]
        for p in ps:
            l_tile = l_tile + p
        acc = acc + jnp.dot(jnp.concatenate(ps, axis=-1).astype(BF16), vp.astype(BF16),
                            preferred_element_type=F32)
    m_scr[...] = m_new
    l_scr[...] = l_tile
    acc_scr[...] = acc


def _sample_attention_output(lam, sg, l_scr, acc_scr, lam_init):
    half = acc_scr.shape[0] // 2
    o = acc_scr[...] / jnp.sum(l_scr[...], axis=-1, keepdims=True)
    w = o[:half] - lam * o[half:]
    return _rms_rows(w, sg) * (1.0 - lam_init)


def _ffn_tail(x_ref, att_ref, conv_ref, gm_ref, wo_ref, fg_ref, wg_ref, wu_ref, wd_ref, y_ref, act_scr, side_work=()):
    mix = (jnp.dot(att_ref[...], wo_ref[0:ATT_W, :], preferred_element_type=F32)
           + jnp.dot(conv_ref[...], wo_ref[ATT_W:ATT_W + CONV_W, :], preferred_element_type=F32)
           + jnp.dot(gm_ref[...], wo_ref[ATT_W + CONV_W:, :], preferred_element_type=F32))
    x1 = x_ref[...] + mix
    h = _rms_rows(x1, fg_ref[...]).astype(BF16)
    chunk_starts = range(0, wg_ref.shape[1], FF_CHUNK)
    assert len(side_work) <= len(chunk_starts)
    for n, c in enumerate(chunk_starts):
        gate = jnp.dot(h, wg_ref[:, c:c + FF_CHUNK], preferred_element_type=F32)
        up = jnp.dot(h, wu_ref[:, c:c + FF_CHUNK], preferred_element_type=F32)
        act_scr[:, c:c + FF_CHUNK] = (_silu(gate) * up).astype(BF16)
        if n < len(side_work):
            side_work[n]()
    y_ref[...] = x1 + jnp.dot(act_scr[...], wd_ref[...], preferred_element_type=F32)


def _tail_kernel(x_ref, att_ref, conv_ref, gm_ref, wo_ref, fg_ref, wg_ref, wu_ref, wd_ref, y_ref, act_scr):
    _ffn_tail(x_ref, att_ref, conv_ref, gm_ref, wo_ref, fg_ref, wg_ref, wu_ref, wd_ref, y_ref, act_scr)


def _tail(x, att, conv, gm, wo, fg, wg, wu, wd, *, tm):
    n, d = x.shape
    d_ff = wg.shape[1]
    assert d_ff % FF_CHUNK == 0
    row = lambda w: pl.BlockSpec((tm, w), lambda i: (i, 0))
    return pl.pallas_call(
        _tail_kernel,
        grid=(n // tm,),
        in_specs=[row(d), row(ATT_W), row(CONV_W), row(GM_W)]
                 + [_const_spec(a.shape) for a in (wo, fg, wg, wu, wd)],
        out_specs=row(d),
        out_shape=jax.ShapeDtypeStruct((n, d), F32),
        scratch_shapes=[pltpu.VMEM((tm, d_ff), BF16)],
        compiler_params=pltpu.CompilerParams(
            dimension_semantics=("arbitrary",), vmem_limit_bytes=V7X_VMEM_LIMIT),
        name="tail",
    )(x, att, conv, gm, wo, fg, wg, wu, wd)


PAGE_GROUP = 8
PAGE_SLOTS = 5
PAGE_LIVE = 2
PAGE_AHEAD = PAGE_SLOTS - PAGE_LIVE


def _tail_attn_kernel(pt_ref, x_ref, att_ref, conv_ref, gm_ref, wo_ref, fg_ref, wg_ref, wu_ref, wd_ref,
                      qx_ref, kn_ref, vn_ref, lq1_ref, lk1_ref, lq2_ref, lk2_ref, sg_ref, ck_hbm, cv_hbm,
                      y_ref, o_ref,
                      act_scr, kbuf, vbuf, sems, m_scr, l_scr, acc_scr, *,
                      layer, lam_init, n_tok, groups_per_step, steps_per_batch):
    s = pl.program_id(0)
    n_slots, group = kbuf.shape[0], kbuf.shape[1]
    ahead = PAGE_AHEAD

    def slot_of(g):
        return lax.rem(s * groups_per_step + g, n_slots)

    def page_copies(g):
        gi = s * groups_per_step + g
        slot = slot_of(g)
        copies = []
        for i in range(group):
            page = pt_ref[gi * group + i]
            copies.append(pltpu.make_async_copy(ck_hbm.at[layer, page], kbuf.at[slot, i], sems.at[0, slot]))
            copies.append(pltpu.make_async_copy(cv_hbm.at[layer, page], vbuf.at[slot, i], sems.at[1, slot]))
        return copies

    @pl.when(s == 0)
    def _():
        for g in range(ahead):
            for cp in page_copies(g):
                cp.start()

    qx = qx_ref[0]

    @pl.when(lax.rem(s, steps_per_batch) == 0)
    def _():
        _self_attention_init(qx, kn_ref, vn_ref, m_scr, l_scr, acc_scr, n_tok)

    def group_scores(g):
        for cp in page_copies(g):
            cp.wait()
        for cp in page_copies(g + ahead):
            cp.start()
        slot = slot_of(g)
        return _page_scores(qx, [kbuf[slot, i] for i in range(group)])

    pending = [group_scores(0)]

    def attend_group(g):
        if g + 1 < groups_per_step:
            pending.append(group_scores(g + 1))
        slot = slot_of(g)
        _attend_pages(pending.pop(0), [vbuf[slot, i] for i in range(group)], m_scr, l_scr, acc_scr)

    _ffn_tail(x_ref, att_ref, conv_ref, gm_ref, wo_ref, fg_ref, wg_ref, wu_ref, wd_ref, y_ref, act_scr,
              side_work=[partial(attend_group, g) for g in range(groups_per_step)])

    @pl.when(lax.rem(s, steps_per_batch) == steps_per_batch - 1)
    def _():
        lam = _diff_lambda(lq1_ref[...], lk1_ref[...], lq2_ref[...], lk2_ref[...], lam_init)
        o_ref[0] = _sample_attention_output(lam, sg_ref[...], l_scr, acc_scr, lam_init)

    @pl.when(s == pl.num_programs(0) - 1)
    def _():
        for g in range(groups_per_step, groups_per_step + ahead):
            for cp in page_copies(g):
                cp.wait()


def _tail_attn(x, att, conv, gm, wo, fg, wg, wu, wd, page_table, qx, kn, vn, lq1, lk1, lq2, lk2, sg,
               cache_k, cache_v, *, tm, layer, lam_init, n_tok):
    n, d = x.shape
    d_ff = wg.shape[1]
    Bd, rows, _ = qx.shape
    n_pages = page_table.shape[1]
    page_rows = cache_k.shape[2]
    n_steps = n // tm
    assert d_ff % FF_CHUNK == 0 and n_steps % Bd == 0
    steps_per_batch = n_steps // Bd
    assert n_pages % (steps_per_batch * PAGE_GROUP) == 0
    groups_per_step = n_pages // (steps_per_batch * PAGE_GROUP)
    assert groups_per_step <= d_ff // FF_CHUNK
    pt_flat = jnp.concatenate([page_table.reshape(-1),
                               jnp.zeros((PAGE_AHEAD * PAGE_GROUP,), page_table.dtype)])
    row = lambda w: pl.BlockSpec((tm, w), lambda i, pt: (i, 0))
    vec = lambda a: pl.BlockSpec(a.shape, lambda i, pt: (0, 0))
    per_b = lambda a: pl.BlockSpec((1,) + a.shape[1:], lambda i, pt: (i // steps_per_batch,) + (0,) * (a.ndim - 1))
    hbm = pl.BlockSpec(memory_space=pl.ANY)
    return pl.pallas_call(
        partial(_tail_attn_kernel, layer=layer, lam_init=lam_init, n_tok=n_tok,
                groups_per_step=groups_per_step, steps_per_batch=steps_per_batch),
        grid_spec=pltpu.PrefetchScalarGridSpec(
            num_scalar_prefetch=1, grid=(n_steps,),
            in_specs=[row(d), row(ATT_W), row(CONV_W), row(GM_W)]
                     + [_const_spec(a.shape) for a in (wo, fg, wg, wu, wd)]
                     + [per_b(qx), per_b(kn), per_b(vn), vec(lq1), vec(lk1), vec(lq2), vec(lk2), vec(sg), hbm, hbm],
            out_specs=(row(d), pl.BlockSpec((1, rows // 2, ATT_DV), lambda i, pt: (i // steps_per_batch, 0, 0))),
            scratch_shapes=[pltpu.VMEM((tm, d_ff), BF16),
                            pltpu.VMEM((PAGE_SLOTS, PAGE_GROUP, page_rows, ATT_DV), F32),
                            pltpu.VMEM((PAGE_SLOTS, PAGE_GROUP, page_rows, ATT_DV), F32),
                            pltpu.SemaphoreType.DMA((2, PAGE_SLOTS)),
                            pltpu.VMEM((rows, LANES), F32), pltpu.VMEM((rows, LANES), F32),
                            pltpu.VMEM((rows, ATT_DV), F32)]),
        out_shape=(jax.ShapeDtypeStruct((n, d), F32), jax.ShapeDtypeStruct((Bd, rows // 2, ATT_DV), F32)),
        compiler_params=pltpu.CompilerParams(
            dimension_semantics=("arbitrary",), vmem_limit_bytes=V7X_VMEM_LIMIT),
        name="tail_attn",
    )(pt_flat, x, att, conv, gm, wo, fg, wg, wu, wd, qx, kn, vn, lq1, lk1, lq2, lk2, sg, cache_k, cache_v)


def kernel(x_prompt, x_sample, cache_k, cache_v, state_conv, page_table, mix_norm_g, w_in, q_norm_g, k_norm_g,
           lam_q1, lam_k1, lam_q2, lam_k2, subln_g, conv_w, conv_b, conv_norm_g, conv_norm_b, gm_norm_g,
           gm_norm_b, gm_ws, gm_bs, w_out, ffn_norm_g, w_gate, w_up, w_down):
    B, S, D = x_prompt.shape
    Bd, T, _ = x_sample.shape
    depth = w_in.shape[0]
    n_phys, page = cache_k.shape[1], cache_k.shape[2]
    assert cache_k.shape[3:] == (ATT_HEADS, 2 * ATT_DH) and cache_v.shape[3:] == (ATT_HEADS, ATT_DV)
    assert w_in.shape[-1] == _END and conv_w.shape[1:] == (CONV_K, CONV_W)
    assert gm_ws.shape[1:] == (GM_HEADS, CHUNK, CHUNK) and S % CHUNK == 0 and T <= CHUNK

    ck = cache_k.reshape(depth, n_phys, page * ATT_HEADS, 2 * ATT_DH)
    cv = cache_v.reshape(depth, n_phys, page * ATT_HEADS, ATT_DV)
    row = lambda a: a.reshape(1, -1)

    xp = x_prompt
    xs = jnp.swapaxes(x_sample, 0, 1).reshape(T * Bd, D)

    kv_all = None
    c_p, k_s, v_s, c_s, g_s = [], [], [], [], []
    for l in range(depth):
        lam_init = 0.8 - 0.6 * math.exp(-0.3 * l)
        w_in_b = w_in[l].astype(BF16)
        qg = row(jnp.tile(q_norm_g[l], QK_W // ATT_DH))
        kg = row(jnp.tile(k_norm_g[l], QK_W // ATT_DH))
        lam_vecs = (row(lam_q1[l]), row(lam_k1[l]), row(lam_q2[l]), row(lam_k2[l]))
        sg = row(subln_g[l])
        mix_args = (row(mix_norm_g[l]), w_in_b, qg, kg, conv_w[l], row(conv_b[l]), row(conv_norm_g[l]),
                    row(conv_norm_b[l]), row(gm_norm_g[l]), row(gm_norm_b[l]))
        tail_w = (w_out[l].astype(BF16), row(ffn_norm_g[l]), w_gate[l].astype(BF16), w_up[l].astype(BF16),
                  w_down[l].astype(BF16))

        ws_stack = gm_ws[l].reshape(GM_HEADS * CHUNK, CHUNK)
        bs_rows = jnp.repeat(gm_bs[l].T, GM_DH, axis=1)
        q, k_all, kb, v_all, vb, conv_p, gm_p, cst_p = _prompt_mix(
            xp, *mix_args, ws_stack, bs_rows, ts=1024, layer=l, depth=depth, kv_all=kv_all)
        kv_all = (k_all, v_all)
        att_p = _prompt_attn(q, kb, vb, *lam_vecs, sg, lam_init=lam_init, tq=S)

        st = jnp.swapaxes(state_conv[l], 0, 1).reshape((CONV_K - 1) * Bd, CONV_W)
        wl = jnp.repeat(jnp.transpose(gm_ws[l][:, :T, :T], (1, 2, 0)).reshape(T * T, GM_HEADS), GM_DH, axis=1)
        bl = jnp.repeat(gm_bs[l][:, :T].T, GM_DH, axis=1)
        qs, ks, vs, conv_s, gm_s, cst, vrows = _sample_mix(xs, st, *mix_args, wl, bl, n_tok=T, n_batch=Bd)
        tb = lambda a: jnp.swapaxes(a.reshape(T, Bd, -1), 0, 1)
        k_bt = tb(ks).reshape(Bd, T, ATT_HEADS, 2 * ATT_DH)
        v_bt = tb(vs).reshape(Bd, T, ATT_HEADS, ATT_DV)
        qx = _split_maps(tb(qs).reshape(Bd, T * ATT_HEADS, 2 * ATT_DH)).astype(BF16)
        reps = 2 * T

        xp, att_s = _tail_attn(
            xp.reshape(B * S, D), att_p.reshape(B * S, ATT_W), conv_p.reshape(B * S, CONV_W),
            gm_p.reshape(B * S, GM_W), *tail_w, page_table, qx, jnp.tile(k_bt, (1, 1, reps, 1)),
            jnp.tile(v_bt, (1, 1, reps, 1)), *lam_vecs, sg, ck, cv, tm=512, layer=l, lam_init=lam_init, n_tok=T)
        xp = xp.reshape(B, S, D)
        att_s = jnp.swapaxes(att_s.reshape(Bd, T, ATT_W), 0, 1).reshape(T * Bd, ATT_W).astype(BF16)
        xs = _tail(xs, att_s, conv_s, gm_s, *tail_w, tm=T * Bd)
        c_p.append(cst_p)
        k_s.append(k_bt)
        v_s.append(v_bt)
        c_s.append(jnp.swapaxes(cst.reshape(CONV_K - 1, Bd, CONV_W), 0, 1))
        g_s.append(tb(vrows))

    ys = jnp.swapaxes(xs.reshape(T, Bd, D), 0, 1)
    k_all, v_all = kv_all
    return (xp, ys, k_all.reshape(depth, B, S, ATT_HEADS, 2 * ATT_DH), v_all.reshape(depth, B, S, ATT_HEADS, ATT_DV),
            jnp.stack(c_p),
            jnp.stack(k_s), jnp.stack(v_s), jnp.stack(c_s), jnp.stack(g_s))
```
